```python
import jax, jax.numpy as jnp
from jax import lax
import numpy as np

D_MODEL = 1024
BATCH = 8
SEQ = 2048
DEPTH = 1
DEC_BATCH = 128
DEC_SEQ = 1
PAST_LEN = 16384
PAGE_SIZE = 128

HG_HEADS = 8
HG_DK = 128
HG_DV = D_MODEL // HG_HEADS
HG_KW = HG_HEADS * HG_DK
HG_VW = HG_HEADS * HG_DV
HG_CHUNK = 16
LRU_W = D_MODEL
LRU_BLOCKS = 8
LRU_BW = LRU_W // LRU_BLOCKS
CONV_W = 4
LRU_C = 8.0
D_FF = ((-(-8 * D_MODEL // 3) + 255) // 256) * 256
EPS = 1e-6
_SIZES = (HG_KW, HG_KW, HG_VW, HG_VW, LRU_W, LRU_W, D_MODEL, D_MODEL)
N_IN = sum(_SIZES)

kernel_name = "hgrn2_rglru_gated_parallel_decoder_step"


def _rmsnorm(x, g):
    xf = x.astype(jnp.float32)
    y = xf * lax.rsqrt(jnp.mean(xf * xf, axis=-1, keepdims=True) + EPS) * g.astype(jnp.float32)
    return y.astype(x.dtype)


def _gla_chunk_step(S, xs):
    qc, kc, vc, bc = xs
    C = qc.shape[1]
    causal = jnp.tril(jnp.ones((C, C), bool))[None, :, :, None, None]
    diff = bc[:, :, None] - bc[:, None, :]
    decay = jnp.exp(jnp.where(causal, diff, -jnp.inf))
    scores = jnp.einsum('bthk,bshk,btshk->btsh', qc, kc, decay)
    o = (jnp.einsum('bthk,bhkv->bthv', qc * jnp.exp(bc), S)
         + jnp.einsum('btsh,bshv->bthv', scores, vc))
    b_last = bc[:, -1]
    S_new = (jnp.exp(b_last)[..., None] * S
             + jnp.einsum('bshk,bshv->bhkv', kc * jnp.exp(b_last[:, None] - bc), vc))
    return S_new, o


def _hgrn2(q, k, v, logf, S0, chunk):
    B, T, H, K = q.shape
    V = v.shape[-1]
    n = T // chunk

    def blocks(a):
        return jnp.moveaxis(a.reshape(B, n, chunk, H, a.shape[-1]), 1, 0)

    bc = jnp.cumsum(blocks(logf), axis=2)
    S_last, o = lax.scan(_gla_chunk_step, S0, (blocks(q), blocks(k), blocks(v), bc))
    return jnp.moveaxis(o, 0, 1).reshape(B, T, H, V), S_last


def _causal_conv(x, buf, w, b):
    T = x.shape[1]
    xp = jnp.concatenate([buf.astype(x.dtype), x], axis=1)
    y = b
    for j in range(CONV_W):
        y = y + xp[:, j:j + T] * w[j]
    return y, xp[:, -(CONV_W - 1):]


def _lru_combine(l, r):
    return (l[0] * r[0], r[0] * l[1] + r[1])


def _rglru(xc, ra, ix, log_lambda, h0, fresh):
    log_a = -LRU_C * jax.nn.sigmoid(ra) * jax.nn.softplus(-log_lambda)
    a = jnp.exp(log_a)
    mult = jnp.sqrt(-jnp.expm1(2.0 * log_a))
    if fresh:
        mult = mult.at[:, 0].set(1.0)
    b = mult * (jax.nn.sigmoid(ix) * xc)
    b = b.at[:, 0].add(a[:, 0] * h0)
    _, h = lax.associative_scan(_lru_combine, (a, b), axis=1)
    return h, h[:, -1]


def _layer(x, S0, h0, cbuf, fresh, chunk, lb,
           n_mix_pre, n_mix_post, n_ffn_pre, n_ffn_post, w_in, hg_onorm,
           conv_w, conv_b, wa, ba, wx, bx, log_lambda, w_pa, w_pb, w_o, w_fi, w_fo):
    B, T, _ = x.shape
    f32 = jnp.float32
    xn = _rmsnorm(x, n_mix_pre)
    proj = xn @ w_in
    idx = [int(s) for s in np.cumsum(_SIZES)[:-1]]
    q, fp, iv, og, lx, ly, ga, gb = jnp.split(proj, idx, axis=-1)

    lbh = lb.reshape(HG_HEADS, HG_DK)
    f = lbh + (1.0 - lbh) * jax.nn.sigmoid(fp.astype(f32).reshape(B, T, HG_HEADS, HG_DK))
    logf = jnp.log(f)
    kk = 1.0 - f
    qh = jax.nn.silu(q.astype(f32)).reshape(B, T, HG_HEADS, HG_DK)
    vh = iv.astype(f32).reshape(B, T, HG_HEADS, HG_DV)
    o, S_new = _hgrn2(qh, kk, vh, logf, S0.astype(f32), chunk)
    o = o * lax.rsqrt(jnp.mean(o * o, axis=-1, keepdims=True) + EPS) * hg_onorm.astype(f32)
    o = o * jax.nn.silu(og.astype(f32).reshape(B, T, HG_HEADS, HG_DV))
    out_a = o.reshape(B, T, HG_VW).astype(x.dtype) @ w_pa

    xc, cbuf_new = _causal_conv(lx, cbuf, conv_w, conv_b)
    xb = xc.reshape(B, T, LRU_BLOCKS, LRU_BW)
    ra = (jnp.einsum('btnc,ncd->btnd', xb, wa) + ba).reshape(B, T, LRU_W).astype(f32)
    ix = (jnp.einsum('btnc,ncd->btnd', xb, wx) + bx).reshape(B, T, LRU_W).astype(f32)
    h, h_last = _rglru(xc.astype(f32), ra, ix, log_lambda.astype(f32), h0.astype(f32), fresh)
    out_b = (jax.nn.gelu(ly) * h.astype(x.dtype)) @ w_pb

    merged = jax.nn.sigmoid(ga) * out_a + jax.nn.sigmoid(gb) * out_b
    x = x + _rmsnorm(merged @ w_o, n_mix_post)

    hn = _rmsnorm(x, n_ffn_pre)
    gate, up = jnp.split(hn @ w_fi, 2, axis=-1)
    x = x + _rmsnorm((jax.nn.silu(gate) * up) @ w_fo, n_ffn_post)
    return x, S_new.astype(S0.dtype), h_last.astype(h0.dtype), cbuf_new.astype(cbuf.dtype)


def setup_inputs(seed: int = 0) -> dict:
    key = jax.random.key(seed)
    ks = jax.random.split(key, 32)
    f32 = jnp.float32

    def nrm(k, shape, scale):
        return jax.random.normal(k, shape, f32) * scale

    def gain(k, shape):
        return 1.0 + 0.05 * jax.random.normal(k, shape, f32)

    a_c = jax.random.uniform(ks[20], (DEPTH, LRU_W), f32, 0.9, 0.999)
    s = a_c ** (1.0 / LRU_C)
    log_lambda = jnp.log(s) - jnp.log1p(-s)
    return {
        "x_prompt": nrm(ks[0], (BATCH, SEQ, D_MODEL), 1.0),
        "x_sample": nrm(ks[1], (DEC_BATCH, DEC_SEQ, D_MODEL), 1.0),
        "state_hgrn": nrm(ks[2], (DEPTH, DEC_BATCH, HG_HEADS, HG_DK, HG_DV), 0.3),
        "state_lru": nrm(ks[3], (DEPTH, DEC_BATCH, LRU_W), 0.5),
        "state_conv": nrm(ks[4], (DEPTH, DEC_BATCH, CONV_W - 1, LRU_W), 1.0),
        "norm_mix_pre": gain(ks[5], (DEPTH, D_MODEL)),
        "norm_mix_post": gain(ks[6], (DEPTH, D_MODEL)),
        "norm_ffn_pre": gain(ks[7], (DEPTH, D_MODEL)),
        "norm_ffn_post": gain(ks[8], (DEPTH, D_MODEL)),
        "w_in": nrm(ks[9], (DEPTH, D_MODEL, N_IN), D_MODEL ** -0.5),
        "hg_lb_logits": nrm(ks[10], (DEPTH + 1, HG_KW), 0.5),
        "hg_out_norm": gain(ks[11], (DEPTH, HG_DV)),
        "lru_conv_w": nrm(ks[12], (DEPTH, CONV_W, LRU_W), CONV_W ** -0.5),
        "lru_conv_b": nrm(ks[13], (DEPTH, LRU_W), 0.02),
        "lru_w_a": nrm(ks[14], (DEPTH, LRU_BLOCKS, LRU_BW, LRU_BW), LRU_BW ** -0.5),
        "lru_b_a": nrm(ks[15], (DEPTH, LRU_BLOCKS, LRU_BW), 0.1),
        "lru_w_x": nrm(ks[16], (DEPTH, LRU_BLOCKS, LRU_BW, LRU_BW), LRU_BW ** -0.5),
        "lru_b_x": nrm(ks[17], (DEPTH, LRU_BLOCKS, LRU_BW), 0.1),
        "lru_log_lambda": log_lambda,
        "w_branch_a": nrm(ks[18], (DEPTH, HG_VW, D_MODEL), HG_VW ** -0.5),
        "w_branch_b": nrm(ks[19], (DEPTH, LRU_W, D_MODEL), LRU_W ** -0.5),
        "w_out": nrm(ks[21], (DEPTH, D_MODEL, D_MODEL), D_MODEL ** -0.5),
        "w_ffn_in": nrm(ks[22], (DEPTH, D_MODEL, 2 * D_FF), D_MODEL ** -0.5),
        "w_ffn_out": nrm(ks[23], (DEPTH, D_FF, D_MODEL), D_FF ** -0.5),
    }


def reference(x_prompt, x_sample, state_hgrn, state_lru, state_conv,
              norm_mix_pre, norm_mix_post, norm_ffn_pre, norm_ffn_post, w_in,
              hg_lb_logits, hg_out_norm, lru_conv_w, lru_conv_b, lru_w_a, lru_b_a,
              lru_w_x, lru_b_x, lru_log_lambda, w_branch_a, w_branch_b, w_out,
              w_ffn_in, w_ffn_out):
    lb_all = jnp.cumsum(jax.nn.softmax(hg_lb_logits.astype(jnp.float32), axis=0), axis=0)
    yp, ys = x_prompt, x_sample
    hg_p, lru_p, cv_p, hg_s, lru_s, cv_s = [], [], [], [], [], []
    for l in range(DEPTH):
        lp = (norm_mix_pre[l], norm_mix_post[l], norm_ffn_pre[l], norm_ffn_post[l], w_in[l],
              hg_out_norm[l], lru_conv_w[l], lru_conv_b[l], lru_w_a[l], lru_b_a[l],
              lru_w_x[l], lru_b_x[l], lru_log_lambda[l], w_branch_a[l], w_branch_b[l],
              w_out[l], w_ffn_in[l], w_ffn_out[l])
        S0p = jnp.zeros((BATCH, HG_HEADS, HG_DK, HG_DV), state_hgrn.dtype)
        h0p = jnp.zeros((BATCH, LRU_W), state_lru.dtype)
        c0p = jnp.zeros((BATCH, CONV_W - 1, LRU_W), state_conv.dtype)
        yp, a1, a2, a3 = _layer(yp, S0p, h0p, c0p, True, HG_CHUNK, lb_all[l], *lp)
        ys, b1, b2, b3 = _layer(ys, state_hgrn[l], state_lru[l], state_conv[l], False,
                                ys.shape[1], lb_all[l], *lp)
        hg_p.append(a1); lru_p.append(a2); cv_p.append(a3)
        hg_s.append(b1); lru_s.append(b2); cv_s.append(b3)
    return (yp, ys, jnp.stack(hg_p), jnp.stack(lru_p), jnp.stack(cv_p),
            jnp.stack(hg_s), jnp.stack(lru_s), jnp.stack(cv_s))
```

```python
import functools

import jax
import jax.numpy as jnp
from jax import lax
from jax.experimental import pallas as pl
from jax.experimental.pallas import tpu as pltpu

F32 = jnp.float32
BF16 = jnp.bfloat16

HEADS = 8
HEAD_DIM = 128
LRU_BLOCKS = 8
LRU_BLOCK_W = 128
CONV_W = 4
LRU_C = 8.0
EPS = 1e-6
CHUNK = 128
SUB = 16
V7X_VMEM_LIMIT_BYTES = 56 * 1024 * 1024


def _rms(x, g):
    ms = jnp.mean(x * x, axis=-1, keepdims=True)
    return x * lax.rsqrt(ms + EPS) * g


def _bdot(a, w):
    return jnp.dot(a.astype(BF16), w, preferred_element_type=F32)


def _dot_nt(a, b):
    return lax.dot_general(a, b, (((1,), (1,)), ((), ())), preferred_element_type=F32)


def _dot_tn(a, b):
    return lax.dot_general(a, b, (((0,), (0,)), ((), ())), preferred_element_type=F32)


def _silu(x):
    return x * jax.nn.sigmoid(x)


def _gelu_tanh(x):
    c = 0.7978845608028654
    return x * (0.5 * (1.0 + jnp.tanh(c * (x + 0.044715 * (x * x * x)))))


def _softplus(z):
    return jnp.maximum(z, 0.0) + jnp.log1p(jnp.exp(-jnp.abs(z)))


def _forget_lower_bound(logits):
    m = jnp.max(logits, axis=0, keepdims=True)
    e = jnp.exp(logits - m)
    return e[0:1] / (e[0:1] + e[1:2])


def _hgrn_gates(q, fp, lb):
    f = lb + (1.0 - lb) * jax.nn.sigmoid(fp)
    return _silu(q), f, 1.0 - f


def _lru_gates(xc, wa_ref, wx_ref, ba, bx, log_lambda):
    ra, ix = [], []
    for n in range(LRU_BLOCKS):
        sl = slice(n * LRU_BLOCK_W, (n + 1) * LRU_BLOCK_W)
        xb = xc[:, sl].astype(BF16)
        ra.append(jnp.dot(xb, wa_ref[n], preferred_element_type=F32))
        ix.append(jnp.dot(xb, wx_ref[n], preferred_element_type=F32))
    ra = jnp.concatenate(ra, axis=-1) + ba
    ix = jnp.concatenate(ix, axis=-1) + bx
    log_a = (-LRU_C) * jax.nn.sigmoid(ra) * _softplus(-log_lambda)
    a = jnp.exp(log_a)
    mult = jnp.sqrt(-jnp.tanh(log_a) * (a * a + 1.0))
    gate_in = jax.nn.sigmoid(ix) * xc
    return a, mult, gate_in


def _mixer_tail(x, o, og, hb, ga, gb, onorm, g_post, w_pa_ref, w_pb_ref, w_o_ref):
    parts = []
    for h in range(HEADS):
        sl = slice(h * HEAD_DIM, (h + 1) * HEAD_DIM)
        oh = o[:, sl]
        ms = jnp.mean(oh * oh, axis=-1, keepdims=True)
        parts.append(oh * lax.rsqrt(ms + EPS) * onorm * _silu(og[:, sl]))
    oa = jnp.concatenate(parts, axis=-1)
    out_a = _bdot(oa, w_pa_ref[...])
    out_b = _bdot(hb, w_pb_ref[...])
    merged = jax.nn.sigmoid(ga) * out_a + jax.nn.sigmoid(gb) * out_b
    return x + _rms(_bdot(merged, w_o_ref[...]), g_post)


def _ffn(x, g_pre, g_post, w_fi_ref, w_fo_ref):
    d_ff = w_fo_ref.shape[0]
    hn = _rms(x, g_pre).astype(BF16)
    gate = jnp.dot(hn, w_fi_ref[:, 0:d_ff], preferred_element_type=F32)
    up = jnp.dot(hn, w_fi_ref[:, d_ff:2 * d_ff], preferred_element_type=F32)
    y = _bdot(_silu(gate) * up, w_fo_ref[...])
    return x + _rms(y, g_post)


def _prompt_mixer_kernel(
        x_ref, g_pre_ref, g_post_ref, w_in_ref, lbl_ref, onorm_ref, cw_ref, cb_ref,
        wa_ref, ba_ref, wx_ref, bx_ref, ll_ref, w_pa_ref, w_pb_ref, w_o_ref,
        y_ref, s_out_ref, h_out_ref, c_out_ref,
        st_ref, h_ref, xp_ref, o_ref, *, tm):
    j = pl.program_id(1)
    d = x_ref.shape[-1]

    @pl.when(j == 0)
    def _():
        st_ref[...] = jnp.zeros_like(st_ref)
        h_ref[...] = jnp.zeros_like(h_ref)
        xp_ref[0:8, :] = jnp.zeros((8, d), F32)

    x = x_ref[0]
    xn = _rms(x, g_pre_ref[...]).astype(BF16)

    def proj(i):
        return jnp.dot(xn, w_in_ref[:, i * d:(i + 1) * d], preferred_element_type=F32)

    row = lax.broadcasted_iota(jnp.int32, (tm, d), 0)

    lb = _forget_lower_bound(lbl_ref[...])
    qh, f, kk = _hgrn_gates(proj(0), proj(1), lb)
    v = proj(2)
    b = jnp.log(f)
    rowc = row & (CHUNK - 1)
    s = 1
    while s < CHUNK:
        b = b + jnp.where(rowc >= s, pltpu.roll(b, s, 0), 0.0)
        s *= 2

    for c in range(tm // CHUNK):
        r0 = c * CHUNK
        for h in range(HEADS):
            sl = slice(h * HEAD_DIM, (h + 1) * HEAD_DIM)
            bc = b[r0:r0 + CHUNK, sl]
            qc = qh[r0:r0 + CHUNK, sl]
            kc = kk[r0:r0 + CHUNK, sl]
            vc = v[r0:r0 + CHUNK, sl].astype(BF16)
            b_last = bc[CHUNK - 1:CHUNK]
            st = st_ref[h]
            o_c = _dot_nt((qc * jnp.exp(bc)).astype(BF16), st.astype(BF16))
            o_rows = [o_c[i * SUB:(i + 1) * SUB] for i in range(CHUNK // SUB)]
            for jb in range(CHUNK // SUB):
                t0 = jb * SUB
                ref_b = bc[t0 + SUB // 2 - 1:t0 + SUB // 2]
                qs = (qc[t0:] * jnp.exp(bc[t0:] - ref_b)).astype(BF16)
                ks = (kc[t0:t0 + SUB] * jnp.exp(ref_b - bc[t0:t0 + SUB])).astype(BF16)
                a_blk = _dot_nt(qs, ks)
                ti = lax.broadcasted_iota(jnp.int32, a_blk.shape, 0)
                si = lax.broadcasted_iota(jnp.int32, a_blk.shape, 1)
                a_blk = jnp.where(ti >= si, a_blk, 0.0).astype(BF16)
                contrib = jnp.dot(a_blk, vc[t0:t0 + SUB], preferred_element_type=F32)
                for i in range(jb, CHUNK // SUB):
                    o_rows[i] = o_rows[i] + contrib[(i - jb) * SUB:(i - jb + 1) * SUB]
            o_ref[r0:r0 + CHUNK, sl] = jnp.concatenate(o_rows, axis=0)
            kst = (kc * jnp.exp(b_last - bc)).astype(BF16)
            st_ref[h] = jnp.exp(b_last) * st + _dot_tn(vc, kst)

    lx = proj(4)
    xp_ref[8:8 + tm, :] = lx
    cw = cw_ref[...]
    xc = cb_ref[...] + xp_ref[5:5 + tm, :] * cw[0:1]
    xc = xc + xp_ref[6:6 + tm, :] * cw[1:2]
    xc = xc + xp_ref[7:7 + tm, :] * cw[2:3]
    xc = xc + lx * cw[3:4]
    xp_ref[0:8, :] = xp_ref[tm:tm + 8, :]

    a, mult, gate_in = _lru_gates(xc, wa_ref, wx_ref, ba_ref[...], bx_ref[...], ll_ref[...])
    mult = jnp.where(jnp.logical_and(row == 0, j == 0), 1.0, mult)
    bb = mult * gate_in
    bb = bb + jnp.where(row == 0, a * h_ref[...], 0.0)
    s = 1
    while s < tm:
        keep = row >= s
        bb = jnp.where(keep, a * pltpu.roll(bb, s, 0) + bb, bb)
        a = jnp.where(keep, a * pltpu.roll(a, s, 0), a)
        s *= 2
    hseq = bb
    h_ref[...] = hseq[tm - 1:tm]
    hb = _gelu_tanh(proj(5)) * hseq

    y_ref[0] = _mixer_tail(x, o_ref[...], proj(3), hb, proj(6), proj(7), onorm_ref[...],
                           g_post_ref[...], w_pa_ref, w_pb_ref, w_o_ref)

    @pl.when(j == pl.num_programs(1) - 1)
    def _():
        for h in range(HEADS):
            s_out_ref[0, h] = st_ref[h].T
        h_out_ref[0] = h_ref[...]
        c_out_ref[0] = xp_ref[5:8, :]


def _full(shape):
    return pl.BlockSpec(shape, lambda *_: (0,) * len(shape))


def _prompt_mixer(x, p, tm):
    nb, t, d = x.shape
    assert t % tm == 0 and tm % CHUNK == 0 and d == HEADS * HEAD_DIM
    kern = functools.partial(_prompt_mixer_kernel, tm=tm)
    in_specs = [
        pl.BlockSpec((1, tm, d), lambda s, j: (s, j, 0)),
        _full((1, d)), _full((1, d)), _full(p["w_in"].shape), _full(p["lb_logits"].shape),
        _full((1, HEAD_DIM)), _full((CONV_W, d)), _full((1, d)),
        _full(p["wa"].shape), _full((1, d)), _full(p["wx"].shape), _full((1, d)), _full((1, d)),
        _full((d, d)), _full((d, d)), _full((d, d)),
    ]
    out_shape = (
        jax.ShapeDtypeStruct((nb, t, d), F32),
        jax.ShapeDtypeStruct((nb, HEADS, HEAD_DIM, HEAD_DIM), F32),
        jax.ShapeDtypeStruct((nb, 1, d), F32),
        jax.ShapeDtypeStruct((nb, CONV_W - 1, d), F32),
    )
    out_specs = (
        pl.BlockSpec((1, tm, d), lambda s, j: (s, j, 0)),
        pl.BlockSpec((1, HEADS, HEAD_DIM, HEAD_DIM), lambda s, j: (s, 0, 0, 0)),
        pl.BlockSpec((1, 1, d), lambda s, j: (s, 0, 0)),
        pl.BlockSpec((1, CONV_W - 1, d), lambda s, j: (s, 0, 0)),
    )
    scratch = [
        pltpu.VMEM((HEADS, HEAD_DIM, HEAD_DIM), F32),
        pltpu.VMEM((1, d), F32),
        pltpu.VMEM((tm + 8, d), F32),
        pltpu.VMEM((tm, d), F32),
    ]
    return pl.pallas_call(
        kern, grid=(nb, t // tm), in_specs=in_specs, out_specs=out_specs, out_shape=out_shape,
        scratch_shapes=scratch, name="prompt_mixer",
        compiler_params=pltpu.CompilerParams(
            dimension_semantics=("arbitrary", "arbitrary"),
            vmem_limit_bytes=V7X_VMEM_LIMIT_BYTES),
    )(x, p["g_mix_pre"], p["g_mix_post"], p["w_in"], p["lb_logits"], p["onorm"], p["conv_w"],
      p["conv_b"], p["wa"], p["ba"], p["wx"], p["bx"], p["log_lambda"], p["w_pa"], p["w_pb"],
      p["w_o"])


def _ffn_kernel(x_ref, g_pre_ref, g_post_ref, w_fi_ref, w_fo_ref, y_ref):
    y_ref[...] = _ffn(x_ref[...], g_pre_ref[...], g_post_ref[...], w_fi_ref, w_fo_ref)


def _ffn_call(x, p, tm):
    n, d = x.shape
    assert n % tm == 0
    return pl.pallas_call(
        _ffn_kernel, grid=(n // tm,),
        in_specs=[pl.BlockSpec((tm, d), lambda i: (i, 0)), _full((1, d)), _full((1, d)),
                  _full(p["w_fi"].shape), _full(p["w_fo"].shape)],
        out_specs=pl.BlockSpec((tm, d), lambda i: (i, 0)),
        out_shape=jax.ShapeDtypeStruct((n, d), F32), name="swiglu",
        compiler_params=pltpu.CompilerParams(
            dimension_semantics=("arbitrary",), vmem_limit_bytes=V7X_VMEM_LIMIT_BYTES),
    )(x, p["g_ffn_pre"], p["g_ffn_post"], p["w_fi"], p["w_fo"])


def _sample_token_kernel(
        x_ref, h0_ref, cbuf_ref, g_pre_ref, w_in_ref, lbl_ref, cw_ref, cb_ref,
        wa_ref, ba_ref, wx_ref, bx_ref, ll_ref,
        ft_ref, kt_ref, qt_ref, v_ref, og_ref, hb_ref, ga_ref, gb_ref, h_out_ref, c_out_ref):
    d = x_ref.shape[-1]
    xn = _rms(x_ref[...], g_pre_ref[...]).astype(BF16)

    def proj(i):
        return jnp.dot(xn, w_in_ref[:, i * d:(i + 1) * d], preferred_element_type=F32)

    qh, f, kk = _hgrn_gates(proj(0), proj(1), _forget_lower_bound(lbl_ref[...]))
    for h in range(HEADS):
        sl = slice(h * HEAD_DIM, (h + 1) * HEAD_DIM)
        ft_ref[sl, :] = f[:, sl].T
        kt_ref[sl, :] = kk[:, sl].T
        qt_ref[sl, :] = qh[:, sl].T
    v_ref[...] = proj(2)
    og_ref[...] = proj(3)

    lx = proj(4)
    cw = cw_ref[...]
    xc = cb_ref[...] + cbuf_ref[0] * cw[0:1]
    xc = xc + cbuf_ref[1] * cw[1:2]
    xc = xc + cbuf_ref[2] * cw[2:3]
    xc = xc + lx * cw[3:4]
    c_out_ref[0] = cbuf_ref[1]
    c_out_ref[1] = cbuf_ref[2]
    c_out_ref[2] = lx
    a, mult, gate_in = _lru_gates(xc, wa_ref, wx_ref, ba_ref[...], bx_ref[...], ll_ref[...])
    hnew = a * h0_ref[...] + mult * gate_in
    h_out_ref[...] = hnew
    hb_ref[...] = _gelu_tanh(proj(5)) * hnew
    ga_ref[...] = proj(6)
    gb_ref[...] = proj(7)


def _sample_state_kernel(s_ref, ft_ref, kt_ref, qt_ref, v_ref, s_out_ref, o_ref, *, bb):
    i = pl.program_id(0)
    lanes = ft_ref.shape[-1]
    shift = lax.rem(lanes - i * bb, lanes)
    ft = pltpu.roll(ft_ref[...], shift, 1)
    kt = pltpu.roll(kt_ref[...], shift, 1)
    qt = pltpu.roll(qt_ref[...], shift, 1)
    for u in range(bb):
        for h in range(HEADS):
            sl = slice(h * HEAD_DIM, (h + 1) * HEAD_DIM)
            s_new = ft[sl, u:u + 1] * s_ref[u, h] + kt[sl, u:u + 1] * v_ref[u:u + 1, sl]
            s_out_ref[u, h] = s_new
            o_ref[u:u + 1, sl] = jnp.sum(qt[sl, u:u + 1] * s_new, axis=0, keepdims=True)


def _sample_tail_kernel(
        x_ref, o_ref, og_ref, hb_ref, ga_ref, gb_ref, onorm_ref, g_post_ref, w_pa_ref, w_pb_ref,
        w_o_ref, g_fpre_ref, g_fpost_ref, w_fi_ref, w_fo_ref, y_ref):
    x1 = _mixer_tail(x_ref[...], o_ref[...], og_ref[...], hb_ref[...], ga_ref[...], gb_ref[...],
                     onorm_ref[...], g_post_ref[...], w_pa_ref, w_pb_ref, w_o_ref)
    y_ref[...] = _ffn(x1, g_fpre_ref[...], g_fpost_ref[...], w_fi_ref, w_fo_ref)


def _sample_step(x, s0, h0, cbuf, p, bb=8):
    nb, d = x.shape
    assert nb == HEAD_DIM and nb % bb == 0
    cp = pltpu.CompilerParams(vmem_limit_bytes=V7X_VMEM_LIMIT_BYTES)
    tok = jax.ShapeDtypeStruct((nb, d), F32)
    col = jax.ShapeDtypeStruct((d, nb), F32)
    ft, kt, qt, v, og, hb, ga, gb, h_new, c_new = pl.pallas_call(
        _sample_token_kernel,
        out_shape=(col, col, col, tok, tok, tok, tok, tok, tok,
                   jax.ShapeDtypeStruct((CONV_W - 1, nb, d), F32)),
        name="sample_token", compiler_params=cp,
    )(x, h0, cbuf, p["g_mix_pre"], p["w_in"], p["lb_logits"], p["conv_w"], p["conv_b"],
      p["wa"], p["ba"], p["wx"], p["bx"], p["log_lambda"])

    s_new, o = pl.pallas_call(
        functools.partial(_sample_state_kernel, bb=bb), grid=(nb // bb,),
        in_specs=[pl.BlockSpec((bb, HEADS, HEAD_DIM, HEAD_DIM), lambda i: (i, 0, 0, 0)),
                  _full((d, nb)), _full((d, nb)), _full((d, nb)),
                  pl.BlockSpec((bb, d), lambda i: (i, 0))],
        out_specs=(pl.BlockSpec((bb, HEADS, HEAD_DIM, HEAD_DIM), lambda i: (i, 0, 0, 0)),
                   pl.BlockSpec((bb, d), lambda i: (i, 0))),
        out_shape=(jax.ShapeDtypeStruct(s0.shape, F32), tok),
        name="sample_state",
        compiler_params=pltpu.CompilerParams(
            dimension_semantics=("arbitrary",), vmem_limit_bytes=V7X_VMEM_LIMIT_BYTES),
    )(s0, ft, kt, qt, v)

    y = pl.pallas_call(
        _sample_tail_kernel, out_shape=tok, name="sample_tail", compiler_params=cp,
    )(x, o, og, hb, ga, gb, p["onorm"], p["g_mix_post"], p["w_pa"], p["w_pb"], p["w_o"],
      p["g_ffn_pre"], p["g_ffn_post"], p["w_fi"], p["w_fo"])
    return y, s_new, h_new, c_new


def kernel(x_prompt, x_sample, state_hgrn, state_lru, state_conv, norm_mix_pre, norm_mix_post,
           norm_ffn_pre, norm_ffn_post, w_in, hg_lb_logits, hg_out_norm, lru_conv_w, lru_conv_b,
           lru_w_a, lru_b_a, lru_w_x, lru_b_x, lru_log_lambda, w_branch_a, w_branch_b, w_out,
           w_ffn_in, w_ffn_out):
    depth = w_in.shape[0]
    assert depth == 1 and hg_lb_logits.shape[0] == 2
    nb, t, d = x_prompt.shape
    nbs = x_sample.shape[0]
    assert x_sample.shape[1] == 1

    def row(a):
        return a.reshape(1, -1).astype(F32)

    p = dict(
        g_mix_pre=row(norm_mix_pre[0]), g_mix_post=row(norm_mix_post[0]),
        g_ffn_pre=row(norm_ffn_pre[0]), g_ffn_post=row(norm_ffn_post[0]),
        w_in=w_in[0].astype(BF16), lb_logits=hg_lb_logits.astype(F32),
        onorm=row(hg_out_norm[0]), conv_w=lru_conv_w[0].astype(F32), conv_b=row(lru_conv_b[0]),
        wa=lru_w_a[0].astype(BF16), ba=row(lru_b_a[0]), wx=lru_w_x[0].astype(BF16),
        bx=row(lru_b_x[0]), log_lambda=row(lru_log_lambda[0]),
        w_pa=w_branch_a[0].astype(BF16), w_pb=w_branch_b[0].astype(BF16),
        w_o=w_out[0].astype(BF16), w_fi=w_ffn_in[0].astype(BF16), w_fo=w_ffn_out[0].astype(BF16),
    )

    x1, s_p, h_p, c_p = _prompt_mixer(x_prompt, p, tm=128)
    y_p = _ffn_call(x1.reshape(nb * t, d), p, tm=256).reshape(nb, t, d)

    y_s, s_s, h_s, c_s = _sample_step(
        x_sample.reshape(nbs, d), state_hgrn[0], state_lru[0],
        jnp.swapaxes(state_conv[0], 0, 1), p)

    return (y_p, y_s.reshape(nbs, 1, d), s_p[None], h_p.reshape(1, nb, d), c_p[None],
            s_s[None], h_s[None], jnp.swapaxes(c_s, 0, 1)[None])
```

```python
import functools

import jax
import jax.numpy as jnp
from jax import lax
from jax.experimental import pallas as pl
from jax.experimental.pallas import tpu as pltpu

F32 = jnp.float32
BF16 = jnp.bfloat16

HEADS = 8
HEAD_DIM = 128
LRU_BLOCKS = 8
LRU_BLOCK_W = 128
CONV_W = 4
LRU_C = 8.0
EPS = 1e-6
CHUNK = 64
SUB = 16
SLAB = 8
PROJ_COLS = 512
V7X_VMEM_LIMIT_BYTES = 56 * 1024 * 1024


def _rms(x, g):
    ms = jnp.mean(x * x, axis=-1, keepdims=True)
    return x * lax.rsqrt(ms + EPS) * g


def _bdot(a, w):
    return jnp.dot(a.astype(BF16), w, preferred_element_type=F32)


def _dot_nt(a, b):
    return lax.dot_general(a, b, (((1,), (1,)), ((), ())), preferred_element_type=F32)


def _dot_tn(a, b):
    return lax.dot_general(a, b, (((0,), (0,)), ((), ())), preferred_element_type=F32)


def _silu(x):
    return x * jax.nn.sigmoid(x)


def _gelu_tanh(x):
    c = 0.7978845608028654
    return x * (0.5 * (1.0 + jnp.tanh(c * (x + 0.044715 * (x * x * x)))))


def _softplus(z):
    return jnp.maximum(z, 0.0) + jnp.log1p(jnp.exp(-jnp.abs(z)))


def _forget_lower_bound(logits):
    m = jnp.max(logits, axis=0, keepdims=True)
    e = jnp.exp(logits - m)
    return e[0:1] / (e[0:1] + e[1:2])


def _lru_gates(xc, wa_ref, wx_ref, ba, bx, log_lambda):
    ra, ix = [], []
    for n in range(LRU_BLOCKS):
        sl = slice(n * LRU_BLOCK_W, (n + 1) * LRU_BLOCK_W)
        xb = xc[:, sl].astype(BF16)
        ra.append(jnp.dot(xb, wa_ref[n], preferred_element_type=F32))
        ix.append(jnp.dot(xb, wx_ref[n], preferred_element_type=F32))
    ra = jnp.concatenate(ra, axis=-1) + ba
    ix = jnp.concatenate(ix, axis=-1) + bx
    log_a = (-LRU_C) * jax.nn.sigmoid(ra) * _softplus(-log_lambda)
    a = jnp.exp(log_a)
    mult = jnp.sqrt(-jnp.tanh(log_a) * (a * a + 1.0))
    gate_in = jax.nn.sigmoid(ix) * xc
    return a, mult, gate_in


def _head_norm_gate(oh, og_silu, onorm):
    ms = jnp.mean(oh * oh, axis=-1, keepdims=True)
    return oh * lax.rsqrt(ms + EPS) * onorm * og_silu


def _merge_out(x, oa, hb, ga_sig, gb_sig, g_post, w_pa_ref, w_pb_ref, w_o_ref):
    out_a = jnp.dot(oa, w_pa_ref[...], preferred_element_type=F32)
    out_b = jnp.dot(hb, w_pb_ref[...], preferred_element_type=F32)
    merged = ga_sig * out_a + gb_sig * out_b
    return x + _rms(_bdot(merged, w_o_ref[...]), g_post)


def _ffn(x, g_pre, g_post, w_fi_ref, w_fo_ref):
    d_ff = w_fo_ref.shape[0]
    hn = _rms(x, g_pre).astype(BF16)
    gate = jnp.dot(hn, w_fi_ref[:, 0:d_ff], preferred_element_type=F32)
    up = jnp.dot(hn, w_fi_ref[:, d_ff:2 * d_ff], preferred_element_type=F32)
    y = _bdot(_silu(gate) * up, w_fo_ref[...])
    return x + _rms(y, g_post)


def _chunk_prefix_sum(b2_ref, r0, sl, row8):
    slabs, carry = [], None
    for i in range(CHUNK // SLAB):
        vv = b2_ref[r0 + i * SLAB:r0 + (i + 1) * SLAB, sl]
        for s in (1, 2, 4):
            vv = vv + jnp.where(row8 >= s, pltpu.roll(vv, s, 0), 0.0)
        if carry is not None:
            vv = vv + carry
        carry = vv[SLAB - 1:SLAB]
        slabs.append(vv)
    return jnp.concatenate(slabs, axis=0)


def _prompt_mixer_kernel(
        x_ref, g_pre_ref, g_post_ref, w_in_ref, lbl_ref, onorm_ref, cw_ref, cb_ref,
        wa_ref, ba_ref, wx_ref, bx_ref, ll_ref, w_pa_ref, w_pb_ref, w_o_ref,
        y_ref, s_out_ref, h_out_ref, c_out_ref,
        st_ref, h_ref, xp_ref, xn_s, qh_s, b2_s, kk_s, v_s, og_s, ly_s, ga_s, gb_s, hg_s, hs_s,
        qin_s, a_s, u_s, dec_s, la_s, lb_s, *, tm):
    j = pl.program_id(1)
    d = x_ref.shape[-1]
    nchunk = tm // CHUNK
    nsub = CHUNK // SUB

    @pl.when(j == 0)
    def _():
        st_ref[...] = jnp.zeros_like(st_ref)
        h_ref[...] = jnp.zeros_like(h_ref)
        xp_ref[0:SLAB, :] = jnp.zeros((SLAB, d), F32)

    always = j >= 0

    def proj(g, c0):
        return jnp.dot(xn_s[...], w_in_ref[:, g * d + c0:g * d + c0 + PROJ_COLS],
                       preferred_element_type=F32)

    def project_recurrence_inputs():
        xn_s[...] = _rms(x_ref[0], g_pre_ref[...]).astype(BF16)
        for c0 in range(0, d, PROJ_COLS):
            xp_ref[SLAB:SLAB + tm, c0:c0 + PROJ_COLS] = proj(4, c0)
        lb = _forget_lower_bound(lbl_ref[...])
        for c0 in range(0, d, PROJ_COLS):
            cs = slice(c0, c0 + PROJ_COLS)
            qh_s[:, cs] = _silu(proj(0, c0))
            f = lb[:, cs] + (1.0 - lb[:, cs]) * jax.nn.sigmoid(proj(1, c0))
            b2_s[:, cs] = jnp.log2(f)
            kk_s[:, cs] = 1.0 - f
            v_s[:, cs] = proj(2, c0).astype(BF16)

    def rglru_gates():
        cw = cw_ref[...]
        xc = cb_ref[...] + xp_ref[5:5 + tm, :] * cw[0:1]
        xc = xc + xp_ref[6:6 + tm, :] * cw[1:2]
        xc = xc + xp_ref[7:7 + tm, :] * cw[2:3]
        xc = xc + xp_ref[8:8 + tm, :] * cw[3:4]
        xp_ref[0:SLAB, :] = xp_ref[tm:tm + SLAB, :]
        a, mult, gate_in = _lru_gates(xc, wa_ref, wx_ref, ba_ref[...], bx_ref[...], ll_ref[...])
        row = lax.broadcasted_iota(jnp.int32, (tm, d), 0)
        mult = jnp.where(jnp.logical_and(row == 0, j == 0), 1.0, mult)
        la_s[...] = a
        lb_s[...] = mult * gate_in

    def project_gates():
        for c0 in range(0, d, PROJ_COLS):
            cs = slice(c0, c0 + PROJ_COLS)
            og_s[:, cs] = _silu(proj(3, c0))
            ly_s[:, cs] = _gelu_tanh(proj(5, c0))
            ga_s[:, cs] = jax.nn.sigmoid(proj(6, c0))
            gb_s[:, cs] = jax.nn.sigmoid(proj(7, c0))

    def hgrn2_scores():
        row8 = lax.broadcasted_iota(jnp.int32, (SLAB, HEAD_DIM), 0)
        for h in range(HEADS):
            sl = slice(h * HEAD_DIM, (h + 1) * HEAD_DIM)
            for c in range(nchunk):
                r0 = c * CHUNK
                rs = slice(r0, r0 + CHUNK)
                bc = _chunk_prefix_sum(b2_s, r0, sl, row8)
                qc, kc, vc = qh_s[rs, sl], kk_s[rs, sl], v_s[rs, sl]
                b_last = bc[CHUNK - 1:CHUNK]
                qin_s[rs, sl] = (qc * jnp.exp2(bc)).astype(BF16)
                dec_s[c:c + 1, sl] = jnp.exp2(b_last)
                for ib in range(nsub):
                    t0, t1 = ib * SUB, (ib + 1) * SUB
                    ref_b = bc[t0 + SUB // 2 - 1:t0 + SUB // 2]
                    qs = (qc[t0:t1] * jnp.exp2(bc[t0:t1] - ref_b)).astype(BF16)
                    ks = (kc[:t1] * jnp.exp2(ref_b - bc[:t1])).astype(BF16)
                    a_blk = _dot_nt(qs, ks)
                    ti = lax.broadcasted_iota(jnp.int32, a_blk.shape, 0) + t0
                    si = lax.broadcasted_iota(jnp.int32, a_blk.shape, 1)
                    a_s[r0 + t0:r0 + t1, h * HEAD_DIM:h * HEAD_DIM + t1] = (
                        jnp.where(ti >= si, a_blk, 0.0).astype(BF16))
                kst = (kc * jnp.exp2(b_last - bc)).astype(BF16)
                u_s[h * nchunk + c] = _dot_tn(vc, kst)

    def hgrn2_recurrence():
        for h in range(HEADS):
            sl = slice(h * HEAD_DIM, (h + 1) * HEAD_DIM)
            st = st_ref[h]
            for c in range(nchunk):
                r0 = c * CHUNK
                rs = slice(r0, r0 + CHUNK)
                o_c = _dot_nt(qin_s[rs, sl], st.astype(BF16))
                rows = []
                for ib in range(nsub):
                    t0, t1 = ib * SUB, (ib + 1) * SUB
                    a_blk = a_s[r0 + t0:r0 + t1, h * HEAD_DIM:h * HEAD_DIM + t1]
                    rows.append(o_c[t0:t1] + jnp.dot(a_blk, v_s[r0:r0 + t1, sl],
                                                     preferred_element_type=F32))
                hg_s[rs, sl] = jnp.concatenate(rows, axis=0)
                st = dec_s[c:c + 1, sl] * st + u_s[h * nchunk + c]
            st_ref[h] = st

    def rglru_scan():
        row8d = lax.broadcasted_iota(jnp.int32, (SLAB, d), 0)
        hc = h_ref[...]
        for i in range(tm // SLAB):
            a8 = la_s[i * SLAB:(i + 1) * SLAB, :]
            b8 = lb_s[i * SLAB:(i + 1) * SLAB, :]
            for s in (1, 2, 4):
                m = row8d >= s
                b8 = b8 + a8 * jnp.where(m, pltpu.roll(b8, s, 0), 0.0)
                a8 = a8 * jnp.where(m, pltpu.roll(a8, s, 0), 1.0)
            h8 = a8 * hc + b8
            hc = h8[SLAB - 1:SLAB]
            hs_s[i * SLAB:(i + 1) * SLAB, :] = h8
        h_ref[...] = hc

    def merge_out():
        onorm = onorm_ref[...]
        oa = jnp.concatenate(
            [_head_norm_gate(hg_s[:, h * HEAD_DIM:(h + 1) * HEAD_DIM],
                             og_s[:, h * HEAD_DIM:(h + 1) * HEAD_DIM], onorm)
             for h in range(HEADS)], axis=-1).astype(BF16)
        hb = (ly_s[...] * hs_s[...]).astype(BF16)
        y_ref[0] = _merge_out(x_ref[0], oa, hb, ga_s[...], gb_s[...], g_post_ref[...],
                              w_pa_ref, w_pb_ref, w_o_ref)

    @pl.when(always)
    def _():
        project_recurrence_inputs()
        rglru_gates()

    @pl.when(always)
    def _():
        hgrn2_scores()
        project_gates()

    @pl.when(always)
    def _():
        hgrn2_recurrence()
        rglru_scan()

    @pl.when(always)
    def _():
        merge_out()

    @pl.when(j == pl.num_programs(1) - 1)
    def _():
        for h in range(HEADS):
            s_out_ref[0, h] = st_ref[h].T
        h_out_ref[0] = h_ref[...]
        c_out_ref[0] = xp_ref[5:8, :]


def _full(shape):
    return pl.BlockSpec(shape, lambda *_: (0,) * len(shape))


def _prompt_mixer(x, p, tm):
    nb, t, d = x.shape
    assert t % tm == 0 and tm % CHUNK == 0 and d == HEADS * HEAD_DIM and d % PROJ_COLS == 0
    assert tm // CHUNK <= SLAB
    kern = functools.partial(_prompt_mixer_kernel, tm=tm)
    in_specs = [
        pl.BlockSpec((1, tm, d), lambda s, j: (s, j, 0)),
        _full((1, d)), _full((1, d)), _full(p["w_in"].shape), _full(p["lb_logits"].shape),
        _full((1, HEAD_DIM)), _full((CONV_W, d)), _full((1, d)),
        _full(p["wa"].shape), _full((1, d)), _full(p["wx"].shape), _full((1, d)), _full((1, d)),
        _full((d, d)), _full((d, d)), _full((d, d)),
    ]
    out_shape = (
        jax.ShapeDtypeStruct((nb, t, d), F32),
        jax.ShapeDtypeStruct((nb, HEADS, HEAD_DIM, HEAD_DIM), F32),
        jax.ShapeDtypeStruct((nb, 1, d), F32),
        jax.ShapeDtypeStruct((nb, CONV_W - 1, d), F32),
    )
    out_specs = (
        pl.BlockSpec((1, tm, d), lambda s, j: (s, j, 0)),
        pl.BlockSpec((1, HEADS, HEAD_DIM, HEAD_DIM), lambda s, j: (s, 0, 0, 0)),
        pl.BlockSpec((1, 1, d), lambda s, j: (s, 0, 0)),
        pl.BlockSpec((1, CONV_W - 1, d), lambda s, j: (s, 0, 0)),
    )
    tile_f32 = pltpu.VMEM((tm, d), F32)
    tile_bf16 = pltpu.VMEM((tm, d), BF16)
    scratch = [
        pltpu.VMEM((HEADS, HEAD_DIM, HEAD_DIM), F32),
        pltpu.VMEM((1, d), F32),
        pltpu.VMEM((tm + SLAB, d), F32),
        tile_bf16,
        tile_f32, tile_f32, tile_f32,
        tile_bf16,
        tile_f32, tile_f32, tile_f32, tile_f32,
        tile_f32,
        tile_f32,
        tile_bf16,
        tile_bf16,
        pltpu.VMEM((HEADS * (tm // CHUNK), HEAD_DIM, HEAD_DIM), F32),
        pltpu.VMEM((SLAB, d), F32),
        tile_f32, tile_f32,
    ]
    return pl.pallas_call(
        kern, grid=(nb, t // tm), in_specs=in_specs, out_specs=out_specs, out_shape=out_shape,
        scratch_shapes=scratch, name="prompt_mixer",
        compiler_params=pltpu.CompilerParams(
            dimension_semantics=("arbitrary", "arbitrary"),
            vmem_limit_bytes=V7X_VMEM_LIMIT_BYTES),
    )(x, p["g_mix_pre"], p["g_mix_post"], p["w_in"], p["lb_logits"], p["onorm"], p["conv_w"],
      p["conv_b"], p["wa"], p["ba"], p["wx"], p["bx"], p["log_lambda"], p["w_pa"], p["w_pb"],
      p["w_o"])


def _ffn_kernel(x_ref, g_pre_ref, g_post_ref, w_fi_ref, w_fo_ref, y_ref):
    y_ref[...] = _ffn(x_ref[...], g_pre_ref[...], g_post_ref[...], w_fi_ref, w_fo_ref)


def _ffn_call(x, p, tm):
    n, d = x.shape
    assert n % tm == 0
    return pl.pallas_call(
        _ffn_kernel, grid=(n // tm,),
        in_specs=[pl.BlockSpec((tm, d), lambda i: (i, 0)), _full((1, d)), _full((1, d)),
                  _full(p["w_fi"].shape), _full(p["w_fo"].shape)],
        out_specs=pl.BlockSpec((tm, d), lambda i: (i, 0)),
        out_shape=jax.ShapeDtypeStruct((n, d), F32), name="swiglu",
        compiler_params=pltpu.CompilerParams(
            dimension_semantics=("arbitrary",), vmem_limit_bytes=V7X_VMEM_LIMIT_BYTES),
    )(x, p["g_ffn_pre"], p["g_ffn_post"], p["w_fi"], p["w_fo"])


def _sample_token_kernel(
        x_ref, h0_ref, cbuf_ref, g_pre_ref, w_in_ref, lbl_ref, cw_ref, cb_ref,
        wa_ref, ba_ref, wx_ref, bx_ref, ll_ref,
        ft_ref, kt_ref, qt_ref, v_ref, og_ref, hb_ref, ga_ref, gb_ref, h_out_ref, c_out_ref):
    d = x_ref.shape[-1]
    xn = _rms(x_ref[...], g_pre_ref[...]).astype(BF16)

    def proj(i):
        return jnp.dot(xn, w_in_ref[:, i * d:(i + 1) * d], preferred_element_type=F32)

    lb = _forget_lower_bound(lbl_ref[...])
    qh = _silu(proj(0))
    f = lb + (1.0 - lb) * jax.nn.sigmoid(proj(1))
    kk = 1.0 - f
    for h in range(HEADS):
        sl = slice(h * HEAD_DIM, (h + 1) * HEAD_DIM)
        ft_ref[sl, :] = f[:, sl].T
        kt_ref[sl, :] = kk[:, sl].T
        qt_ref[sl, :] = qh[:, sl].T
    v_ref[...] = proj(2)
    og_ref[...] = _silu(proj(3))

    lx = proj(4)
    cw = cw_ref[...]
    xc = cb_ref[...] + cbuf_ref[0] * cw[0:1]
    xc = xc + cbuf_ref[1] * cw[1:2]
    xc = xc + cbuf_ref[2] * cw[2:3]
    xc = xc + lx * cw[3:4]
    c_out_ref[0] = cbuf_ref[1]
    c_out_ref[1] = cbuf_ref[2]
    c_out_ref[2] = lx
    a, mult, gate_in = _lru_gates(xc, wa_ref, wx_ref, ba_ref[...], bx_ref[...], ll_ref[...])
    hnew = a * h0_ref[...] + mult * gate_in
    h_out_ref[...] = hnew
    hb_ref[...] = (_gelu_tanh(proj(5)) * hnew).astype(BF16)
    ga_ref[...] = jax.nn.sigmoid(proj(6))
    gb_ref[...] = jax.nn.sigmoid(proj(7))


def _sample_state_kernel(s_ref, ft_ref, kt_ref, qt_ref, v_ref, s_out_ref, o_ref, *, bb):
    i = pl.program_id(0)
    lanes = ft_ref.shape[-1]
    shift = lax.rem(lanes - i * bb, lanes)
    ft = pltpu.roll(ft_ref[...], shift, 1)
    kt = pltpu.roll(kt_ref[...], shift, 1)
    qt = pltpu.roll(qt_ref[...], shift, 1)
    for u in range(bb):
        for h in range(HEADS):
            sl = slice(h * HEAD_DIM, (h + 1) * HEAD_DIM)
            s_new = ft[sl, u:u + 1] * s_ref[u, h] + kt[sl, u:u + 1] * v_ref[u:u + 1, sl]
            s_out_ref[u, h] = s_new
            o_ref[u:u + 1, sl] = jnp.sum(qt[sl, u:u + 1] * s_new, axis=0, keepdims=True)


def _sample_tail_kernel(
        x_ref, o_ref, og_ref, hb_ref, ga_ref, gb_ref, onorm_ref, g_post_ref, w_pa_ref, w_pb_ref,
        w_o_ref, g_fpre_ref, g_fpost_ref, w_fi_ref, w_fo_ref, y_ref):
    o = o_ref[...]
    og = og_ref[...]
    oa = jnp.concatenate(
        [_head_norm_gate(o[:, h * HEAD_DIM:(h + 1) * HEAD_DIM],
                         og[:, h * HEAD_DIM:(h + 1) * HEAD_DIM], onorm_ref[...])
         for h in range(HEADS)], axis=-1).astype(BF16)
    x1 = _merge_out(x_ref[...], oa, hb_ref[...], ga_ref[...], gb_ref[...], g_post_ref[...],
                    w_pa_ref, w_pb_ref, w_o_ref)
    y_ref[...] = _ffn(x1, g_fpre_ref[...], g_fpost_ref[...], w_fi_ref, w_fo_ref)


def _sample_step(x, s0, h0, cbuf, p, bb=8):
    nb, d = x.shape
    assert nb == HEAD_DIM and nb % bb == 0
    cp = pltpu.CompilerParams(vmem_limit_bytes=V7X_VMEM_LIMIT_BYTES)
    tok = jax.ShapeDtypeStruct((nb, d), F32)
    tok16 = jax.ShapeDtypeStruct((nb, d), BF16)
    col = jax.ShapeDtypeStruct((d, nb), F32)
    ft, kt, qt, v, og, hb, ga, gb, h_new, c_new = pl.pallas_call(
        _sample_token_kernel,
        out_shape=(col, col, col, tok, tok, tok16, tok, tok, tok,
                   jax.ShapeDtypeStruct((CONV_W - 1, nb, d), F32)),
        name="sample_token", compiler_params=cp,
    )(x, h0, cbuf, p["g_mix_pre"], p["w_in"], p["lb_logits"], p["conv_w"], p["conv_b"],
      p["wa"], p["ba"], p["wx"], p["bx"], p["log_lambda"])

    s_new, o = pl.pallas_call(
        functools.partial(_sample_state_kernel, bb=bb), grid=(nb // bb,),
        in_specs=[pl.BlockSpec((bb, HEADS, HEAD_DIM, HEAD_DIM), lambda i: (i, 0, 0, 0)),
                  _full((d, nb)), _full((d, nb)), _full((d, nb)),
                  pl.BlockSpec((bb, d), lambda i: (i, 0))],
        out_specs=(pl.BlockSpec((bb, HEADS, HEAD_DIM, HEAD_DIM), lambda i: (i, 0, 0, 0)),
                   pl.BlockSpec((bb, d), lambda i: (i, 0))),
        out_shape=(jax.ShapeDtypeStruct(s0.shape, F32), tok),
        name="sample_state",
        compiler_params=pltpu.CompilerParams(
            dimension_semantics=("arbitrary",), vmem_limit_bytes=V7X_VMEM_LIMIT_BYTES),
    )(s0, ft, kt, qt, v)

    y = pl.pallas_call(
        _sample_tail_kernel, out_shape=tok, name="sample_tail", compiler_params=cp,
    )(x, o, og, hb, ga, gb, p["onorm"], p["g_mix_post"], p["w_pa"], p["w_pb"], p["w_o"],
      p["g_ffn_pre"], p["g_ffn_post"], p["w_fi"], p["w_fo"])
    return y, s_new, h_new, c_new


def kernel(x_prompt, x_sample, state_hgrn, state_lru, state_conv, norm_mix_pre, norm_mix_post,
           norm_ffn_pre, norm_ffn_post, w_in, hg_lb_logits, hg_out_norm, lru_conv_w, lru_conv_b,
           lru_w_a, lru_b_a, lru_w_x, lru_b_x, lru_log_lambda, w_branch_a, w_branch_b, w_out,
           w_ffn_in, w_ffn_out):
    depth = w_in.shape[0]
    assert depth == 1 and hg_lb_logits.shape[0] == 2
    nb, t, d = x_prompt.shape
    nbs = x_sample.shape[0]
    assert x_sample.shape[1] == 1

    def row(a):
        return a.reshape(1, -1).astype(F32)

    p = dict(
        g_mix_pre=row(norm_mix_pre[0]), g_mix_post=row(norm_mix_post[0]),
        g_ffn_pre=row(norm_ffn_pre[0]), g_ffn_post=row(norm_ffn_post[0]),
        w_in=w_in[0].astype(BF16), lb_logits=hg_lb_logits.astype(F32),
        onorm=row(hg_out_norm[0]), conv_w=lru_conv_w[0].astype(F32), conv_b=row(lru_conv_b[0]),
        wa=lru_w_a[0].astype(BF16), ba=row(lru_b_a[0]), wx=lru_w_x[0].astype(BF16),
        bx=row(lru_b_x[0]), log_lambda=row(lru_log_lambda[0]),
        w_pa=w_branch_a[0].astype(BF16), w_pb=w_branch_b[0].astype(BF16),
        w_o=w_out[0].astype(BF16), w_fi=w_ffn_in[0].astype(BF16), w_fo=w_ffn_out[0].astype(BF16),
    )

    x1, s_p, h_p, c_p = _prompt_mixer(x_prompt, p, tm=256)
    y_p = _ffn_call(x1.reshape(nb * t, d), p, tm=256).reshape(nb, t, d)

    y_s, s_s, h_s, c_s = _sample_step(
        x_sample.reshape(nbs, d), state_hgrn[0], state_lru[0],
        jnp.swapaxes(state_conv[0], 0, 1), p)

    return (y_p, y_s.reshape(nbs, 1, d), s_p[None], h_p.reshape(1, nb, d), c_p[None],
            s_s[None], h_s[None], jnp.swapaxes(c_s, 0, 1)[None])
```

```python
import functools

import jax
import jax.numpy as jnp
from jax import lax
from jax.experimental import pallas as pl
from jax.experimental.pallas import tpu as pltpu

F32 = jnp.float32
BF16 = jnp.bfloat16

HEADS = 8
HEAD_DIM = 128
LRU_BLOCKS = 8
LRU_BLOCK_W = 128
CONV_W = 4
LRU_C = 8.0
EPS = 1e-6
CHUNK = 64
SUB = 16
SLAB = 8
PROJ_COLS = 512
FFN_COLS = 256
V7X_VMEM_LIMIT_BYTES = 56 * 1024 * 1024


def _rms(x, g):
    ms = jnp.mean(x * x, axis=-1, keepdims=True)
    return x * lax.rsqrt(ms + EPS) * g


def _bdot(a, w):
    return jnp.dot(a.astype(BF16), w, preferred_element_type=F32)


def _dot_nt(a, b):
    return lax.dot_general(a, b, (((1,), (1,)), ((), ())), preferred_element_type=F32)


def _dot_tn(a, b):
    return lax.dot_general(a, b, (((0,), (0,)), ((), ())), preferred_element_type=F32)


def _silu(x):
    return x * jax.nn.sigmoid(x)


def _gelu_tanh(x):
    c = 0.7978845608028654
    return x * (0.5 * (1.0 + jnp.tanh(c * (x + 0.044715 * (x * x * x)))))


def _softplus(z):
    return jnp.maximum(z, 0.0) + jnp.log1p(jnp.exp(-jnp.abs(z)))


def _forget_lower_bound(logits):
    m = jnp.max(logits, axis=0, keepdims=True)
    e = jnp.exp(logits - m)
    return e[0:1] / (e[0:1] + e[1:2])


def _lru_gates(xc, wa_ref, wx_ref, ba, bx, log_lambda):
    ra, ix = [], []
    for n in range(LRU_BLOCKS):
        sl = slice(n * LRU_BLOCK_W, (n + 1) * LRU_BLOCK_W)
        xb = xc[:, sl].astype(BF16)
        ra.append(jnp.dot(xb, wa_ref[n], preferred_element_type=F32))
        ix.append(jnp.dot(xb, wx_ref[n], preferred_element_type=F32))
    ra = jnp.concatenate(ra, axis=-1) + ba
    ix = jnp.concatenate(ix, axis=-1) + bx
    log_a = (-LRU_C) * jax.nn.sigmoid(ra) * _softplus(-log_lambda)
    a = jnp.exp(log_a)
    mult = jnp.sqrt(-jnp.tanh(log_a) * (a * a + 1.0))
    gate_in = jax.nn.sigmoid(ix) * xc
    return a, mult, gate_in


def _head_norm_gate(oh, og_silu, onorm):
    ms = jnp.mean(oh * oh, axis=-1, keepdims=True)
    return oh * lax.rsqrt(ms + EPS) * onorm * og_silu


def _split_cols(w, cols):
    k, n = w.shape
    assert n % cols == 0
    return w.reshape(k, n // cols, cols).transpose(1, 0, 2)


def _cdot(a, w_ref, first=0, count=None):
    count = w_ref.shape[0] - first if count is None else count
    parts = [jnp.dot(a, w_ref[first + i], preferred_element_type=F32) for i in range(count)]
    return parts[0] if count == 1 else jnp.concatenate(parts, axis=-1)


def _merge_out(x, oa, hb, ga_sig, gb_sig, g_post, w_pa_ref, w_pb_ref, w_o_ref):
    merged = ga_sig * _cdot(oa, w_pa_ref) + gb_sig * _cdot(hb, w_pb_ref)
    return x + _rms(_cdot(merged.astype(BF16), w_o_ref), g_post)


def _ffn(x, g_pre, g_post, w_fi_ref, w_fo_ref):
    nf = w_fi_ref.shape[0] // 2
    hn = _rms(x, g_pre).astype(BF16)
    act = jnp.concatenate(
        [(_silu(_cdot(hn, w_fi_ref, i, 1)) * _cdot(hn, w_fi_ref, nf + i, 1)).astype(BF16)
         for i in range(nf)], axis=-1)
    return x + _rms(_cdot(act, w_fo_ref), g_post)


def _chunk_prefix_sum(b2_ref, r0, sl, row8):
    slabs, carry = [], None
    for i in range(CHUNK // SLAB):
        vv = b2_ref[r0 + i * SLAB:r0 + (i + 1) * SLAB, sl]
        for s in (1, 2, 4):
            vv = vv + jnp.where(row8 >= s, pltpu.roll(vv, s, 0), 0.0)
        if carry is not None:
            vv = vv + carry
        carry = vv[SLAB - 1:SLAB]
        slabs.append(vv)
    return jnp.concatenate(slabs, axis=0)


def _prompt_mixer_kernel(
        x_ref, g_pre_ref, g_post_ref, w_in_ref, lbl_ref, onorm_ref, cw_ref, cb_ref,
        wa_ref, ba_ref, wx_ref, bx_ref, ll_ref, w_pa_ref, w_pb_ref, w_o_ref,
        y_ref, s_out_ref, h_out_ref, c_out_ref,
        st_ref, h_ref, xp_ref, xn_s, qh_s, b2_s, kk_s, v_s, og_s, ly_s, ga_s, gb_s, hg_s, hs_s,
        qin_s, a_s, u_s, dec_s, la_s, lb_s, *, tm):
    j = pl.program_id(1)
    d = x_ref.shape[-1]
    nchunk = tm // CHUNK
    nsub = CHUNK // SUB

    @pl.when(j == 0)
    def _():
        st_ref[...] = jnp.zeros_like(st_ref)
        h_ref[...] = jnp.zeros_like(h_ref)
        xp_ref[0:SLAB, :] = jnp.zeros((SLAB, d), F32)

    always = j >= 0

    def proj(g, c0):
        return jnp.dot(xn_s[...], w_in_ref[(g * d + c0) // PROJ_COLS],
                       preferred_element_type=F32)

    def project_recurrence_inputs():
        xn_s[...] = _rms(x_ref[0], g_pre_ref[...]).astype(BF16)
        for c0 in range(0, d, PROJ_COLS):
            xp_ref[SLAB:SLAB + tm, c0:c0 + PROJ_COLS] = proj(4, c0)
        lb = _forget_lower_bound(lbl_ref[...])
        for c0 in range(0, d, PROJ_COLS):
            cs = slice(c0, c0 + PROJ_COLS)
            qh_s[:, cs] = _silu(proj(0, c0))
            f = lb[:, cs] + (1.0 - lb[:, cs]) * jax.nn.sigmoid(proj(1, c0))
            b2_s[:, cs] = jnp.log2(f)
            kk_s[:, cs] = 1.0 - f
            v_s[:, cs] = proj(2, c0).astype(BF16)

    def rglru_gates():
        cw = cw_ref[...]
        xc = cb_ref[...] + xp_ref[5:5 + tm, :] * cw[0:1]
        xc = xc + xp_ref[6:6 + tm, :] * cw[1:2]
        xc = xc + xp_ref[7:7 + tm, :] * cw[2:3]
        xc = xc + xp_ref[8:8 + tm, :] * cw[3:4]
        xp_ref[0:SLAB, :] = xp_ref[tm:tm + SLAB, :]
        a, mult, gate_in = _lru_gates(xc, wa_ref, wx_ref, ba_ref[...], bx_ref[...], ll_ref[...])
        row = lax.broadcasted_iota(jnp.int32, (tm, d), 0)
        mult = jnp.where(jnp.logical_and(row == 0, j == 0), 1.0, mult)
        la_s[...] = a
        lb_s[...] = mult * gate_in

    def project_gates():
        for c0 in range(0, d, PROJ_COLS):
            cs = slice(c0, c0 + PROJ_COLS)
            og_s[:, cs] = _silu(proj(3, c0))
            ly_s[:, cs] = _gelu_tanh(proj(5, c0))
            ga_s[:, cs] = jax.nn.sigmoid(proj(6, c0))
            gb_s[:, cs] = jax.nn.sigmoid(proj(7, c0))

    def hgrn2_scores():
        row8 = lax.broadcasted_iota(jnp.int32, (SLAB, HEAD_DIM), 0)
        for h in range(HEADS):
            sl = slice(h * HEAD_DIM, (h + 1) * HEAD_DIM)
            for c in range(nchunk):
                r0 = c * CHUNK
                rs = slice(r0, r0 + CHUNK)
                bc = _chunk_prefix_sum(b2_s, r0, sl, row8)
                qc, kc, vc = qh_s[rs, sl], kk_s[rs, sl], v_s[rs, sl]
                b_last = bc[CHUNK - 1:CHUNK]
                qin_s[rs, sl] = (qc * jnp.exp2(bc)).astype(BF16)
                dec_s[c:c + 1, sl] = jnp.exp2(b_last)
                for ib in range(nsub):
                    t0, t1 = ib * SUB, (ib + 1) * SUB
                    ref_b = bc[t0 + SUB // 2 - 1:t0 + SUB // 2]
                    qs = (qc[t0:t1] * jnp.exp2(bc[t0:t1] - ref_b)).astype(BF16)
                    ks = (kc[:t1] * jnp.exp2(ref_b - bc[:t1])).astype(BF16)
                    a_blk = _dot_nt(qs, ks)
                    ti = lax.broadcasted_iota(jnp.int32, a_blk.shape, 0) + t0
                    si = lax.broadcasted_iota(jnp.int32, a_blk.shape, 1)
                    a_s[r0 + t0:r0 + t1, h * HEAD_DIM:h * HEAD_DIM + t1] = (
                        jnp.where(ti >= si, a_blk, 0.0).astype(BF16))
                kst = (kc * jnp.exp2(b_last - bc)).astype(BF16)
                u_s[h * nchunk + c] = _dot_tn(vc, kst)

    def hgrn2_recurrence():
        for h in range(HEADS):
            sl = slice(h * HEAD_DIM, (h + 1) * HEAD_DIM)
            st = st_ref[h]
            for c in range(nchunk):
                r0 = c * CHUNK
                rs = slice(r0, r0 + CHUNK)
                o_c = _dot_nt(qin_s[rs, sl], st.astype(BF16))
                rows = []
                for ib in range(nsub):
                    t0, t1 = ib * SUB, (ib + 1) * SUB
                    a_blk = a_s[r0 + t0:r0 + t1, h * HEAD_DIM:h * HEAD_DIM + t1]
                    rows.append(o_c[t0:t1] + jnp.dot(a_blk, v_s[r0:r0 + t1, sl],
                                                     preferred_element_type=F32))
                hg_s[rs, sl] = jnp.concatenate(rows, axis=0)
                st = dec_s[c:c + 1, sl] * st + u_s[h * nchunk + c]
            st_ref[h] = st

    def rglru_scan():
        row8d = lax.broadcasted_iota(jnp.int32, (SLAB, d), 0)
        hc = h_ref[...]
        for i in range(tm // SLAB):
            a8 = la_s[i * SLAB:(i + 1) * SLAB, :]
            b8 = lb_s[i * SLAB:(i + 1) * SLAB, :]
            for s in (1, 2, 4):
                m = row8d >= s
                b8 = b8 + a8 * jnp.where(m, pltpu.roll(b8, s, 0), 0.0)
                a8 = a8 * jnp.where(m, pltpu.roll(a8, s, 0), 1.0)
            h8 = a8 * hc + b8
            hc = h8[SLAB - 1:SLAB]
            hs_s[i * SLAB:(i + 1) * SLAB, :] = h8
        h_ref[...] = hc

    def merge_out():
        onorm = onorm_ref[...]
        oa = jnp.concatenate(
            [_head_norm_gate(hg_s[:, h * HEAD_DIM:(h + 1) * HEAD_DIM],
                             og_s[:, h * HEAD_DIM:(h + 1) * HEAD_DIM], onorm)
             for h in range(HEADS)], axis=-1).astype(BF16)
        hb = (ly_s[...] * hs_s[...]).astype(BF16)
        y_ref[0] = _merge_out(x_ref[0], oa, hb, ga_s[...], gb_s[...], g_post_ref[...],
                              w_pa_ref, w_pb_ref, w_o_ref)

    @pl.when(always)
    def _():
        project_recurrence_inputs()
        rglru_gates()

    @pl.when(always)
    def _():
        hgrn2_scores()
        project_gates()

    @pl.when(always)
    def _():
        hgrn2_recurrence()
        rglru_scan()

    @pl.when(always)
    def _():
        merge_out()

    @pl.when(j == pl.num_programs(1) - 1)
    def _():
        for h in range(HEADS):
            s_out_ref[0, h] = st_ref[h].T
        h_out_ref[0] = h_ref[...]
        c_out_ref[0] = xp_ref[5:8, :]


def _full(shape):
    return pl.BlockSpec(shape, lambda *_: (0,) * len(shape))


def _prompt_mixer(x, p, tm):
    nb, t, d = x.shape
    assert t % tm == 0 and tm % CHUNK == 0 and d == HEADS * HEAD_DIM and d % PROJ_COLS == 0
    assert tm // CHUNK <= SLAB
    kern = functools.partial(_prompt_mixer_kernel, tm=tm)
    in_specs = [
        pl.BlockSpec((1, tm, d), lambda s, j: (s, j, 0)),
        _full((1, d)), _full((1, d)), _full(p["w_in"].shape), _full(p["lb_logits"].shape),
        _full((1, HEAD_DIM)), _full((CONV_W, d)), _full((1, d)),
        _full(p["wa"].shape), _full((1, d)), _full(p["wx"].shape), _full((1, d)), _full((1, d)),
        _full(p["w_pa"].shape), _full(p["w_pb"].shape), _full(p["w_o"].shape),
    ]
    out_shape = (
        jax.ShapeDtypeStruct((nb, t, d), F32),
        jax.ShapeDtypeStruct((nb, HEADS, HEAD_DIM, HEAD_DIM), F32),
        jax.ShapeDtypeStruct((nb, 1, d), F32),
        jax.ShapeDtypeStruct((nb, CONV_W - 1, d), F32),
    )
    out_specs = (
        pl.BlockSpec((1, tm, d), lambda s, j: (s, j, 0)),
        pl.BlockSpec((1, HEADS, HEAD_DIM, HEAD_DIM), lambda s, j: (s, 0, 0, 0)),
        pl.BlockSpec((1, 1, d), lambda s, j: (s, 0, 0)),
        pl.BlockSpec((1, CONV_W - 1, d), lambda s, j: (s, 0, 0)),
    )
    tile_f32 = pltpu.VMEM((tm, d), F32)
    tile_bf16 = pltpu.VMEM((tm, d), BF16)
    scratch = [
        pltpu.VMEM((HEADS, HEAD_DIM, HEAD_DIM), F32),
        pltpu.VMEM((1, d), F32),
        pltpu.VMEM((tm + SLAB, d), F32),
        tile_bf16,
        tile_f32, tile_f32, tile_f32,
        tile_bf16,
        tile_f32, tile_f32, tile_f32, tile_f32,
        tile_f32,
        tile_f32,
        tile_bf16,
        tile_bf16,
        pltpu.VMEM((HEADS * (tm // CHUNK), HEAD_DIM, HEAD_DIM), F32),
        pltpu.VMEM((SLAB, d), F32),
        tile_f32, tile_f32,
    ]
    return pl.pallas_call(
        kern, grid=(nb, t // tm), in_specs=in_specs, out_specs=out_specs, out_shape=out_shape,
        scratch_shapes=scratch, name="prompt_mixer",
        compiler_params=pltpu.CompilerParams(
            dimension_semantics=("arbitrary", "arbitrary"),
            vmem_limit_bytes=V7X_VMEM_LIMIT_BYTES),
    )(x, p["g_mix_pre"], p["g_mix_post"], p["w_in"], p["lb_logits"], p["onorm"], p["conv_w"],
      p["conv_b"], p["wa"], p["ba"], p["wx"], p["bx"], p["log_lambda"], p["w_pa"], p["w_pb"],
      p["w_o"])


def _ffn_kernel(x_ref, g_pre_ref, g_post_ref, w_fi_ref, w_fo_ref, y_ref):
    y_ref[...] = _ffn(x_ref[...], g_pre_ref[...], g_post_ref[...], w_fi_ref, w_fo_ref)


def _ffn_call(x, p, tm):
    n, d = x.shape
    assert n % tm == 0
    return pl.pallas_call(
        _ffn_kernel, grid=(n // tm,),
        in_specs=[pl.BlockSpec((tm, d), lambda i: (i, 0)), _full((1, d)), _full((1, d)),
                  _full(p["w_fi"].shape), _full(p["w_fo"].shape)],
        out_specs=pl.BlockSpec((tm, d), lambda i: (i, 0)),
        out_shape=jax.ShapeDtypeStruct((n, d), F32), name="swiglu",
        compiler_params=pltpu.CompilerParams(
            dimension_semantics=("arbitrary",), vmem_limit_bytes=V7X_VMEM_LIMIT_BYTES),
    )(x, p["g_ffn_pre"], p["g_ffn_post"], p["w_fi"], p["w_fo"])


def _sample_token_kernel(
        x_ref, h0_ref, cbuf_ref, g_pre_ref, w_in_ref, lbl_ref, cw_ref, cb_ref,
        wa_ref, ba_ref, wx_ref, bx_ref, ll_ref,
        ft_ref, kt_ref, qt_ref, v_ref, og_ref, hb_ref, ga_ref, gb_ref, h_out_ref, c_out_ref):
    d = x_ref.shape[-1]
    xn = _rms(x_ref[...], g_pre_ref[...]).astype(BF16)

    def proj(i):
        per = d // PROJ_COLS
        return _cdot(xn, w_in_ref, i * per, per)

    lb = _forget_lower_bound(lbl_ref[...])
    qh = _silu(proj(0))
    f = lb + (1.0 - lb) * jax.nn.sigmoid(proj(1))
    kk = 1.0 - f
    for h in range(HEADS):
        sl = slice(h * HEAD_DIM, (h + 1) * HEAD_DIM)
        ft_ref[sl, :] = f[:, sl].T
        kt_ref[sl, :] = kk[:, sl].T
        qt_ref[sl, :] = qh[:, sl].T
    v_ref[...] = proj(2)
    og_ref[...] = _silu(proj(3))

    lx = proj(4)
    cw = cw_ref[...]
    xc = cb_ref[...] + cbuf_ref[0] * cw[0:1]
    xc = xc + cbuf_ref[1] * cw[1:2]
    xc = xc + cbuf_ref[2] * cw[2:3]
    xc = xc + lx * cw[3:4]
    c_out_ref[0] = cbuf_ref[1]
    c_out_ref[1] = cbuf_ref[2]
    c_out_ref[2] = lx
    a, mult, gate_in = _lru_gates(xc, wa_ref, wx_ref, ba_ref[...], bx_ref[...], ll_ref[...])
    hnew = a * h0_ref[...] + mult * gate_in
    h_out_ref[...] = hnew
    hb_ref[...] = (_gelu_tanh(proj(5)) * hnew).astype(BF16)
    ga_ref[...] = jax.nn.sigmoid(proj(6))
    gb_ref[...] = jax.nn.sigmoid(proj(7))


def _sample_state_kernel(s_ref, ft_ref, kt_ref, qt_ref, v_ref, s_out_ref, o_ref, *, bb):
    i = pl.program_id(0)
    lanes = ft_ref.shape[-1]
    shift = lax.rem(lanes - i * bb, lanes)
    ft = pltpu.roll(ft_ref[...], shift, 1)
    kt = pltpu.roll(kt_ref[...], shift, 1)
    qt = pltpu.roll(qt_ref[...], shift, 1)
    for u in range(bb):
        for h in range(HEADS):
            sl = slice(h * HEAD_DIM, (h + 1) * HEAD_DIM)
            s_new = ft[sl, u:u + 1] * s_ref[u, h] + kt[sl, u:u + 1] * v_ref[u:u + 1, sl]
            s_out_ref[u, h] = s_new
            o_ref[u:u + 1, sl] = jnp.sum(qt[sl, u:u + 1] * s_new, axis=0, keepdims=True)


def _sample_tail_kernel(
        x_ref, o_ref, og_ref, hb_ref, ga_ref, gb_ref, onorm_ref, g_post_ref, w_pa_ref, w_pb_ref,
        w_o_ref, g_fpre_ref, g_fpost_ref, w_fi_ref, w_fo_ref, y_ref):
    o = o_ref[...]
    og = og_ref[...]
    oa = jnp.concatenate(
        [_head_norm_gate(o[:, h * HEAD_DIM:(h + 1) * HEAD_DIM],
                         og[:, h * HEAD_DIM:(h + 1) * HEAD_DIM], onorm_ref[...])
         for h in range(HEADS)], axis=-1).astype(BF16)
    x1 = _merge_out(x_ref[...], oa, hb_ref[...], ga_ref[...], gb_ref[...], g_post_ref[...],
                    w_pa_ref, w_pb_ref, w_o_ref)
    y_ref[...] = _ffn(x1, g_fpre_ref[...], g_fpost_ref[...], w_fi_ref, w_fo_ref)


def _sample_step(x, s0, h0, cbuf, p, bb=8):
    nb, d = x.shape
    assert nb == HEAD_DIM and nb % bb == 0
    cp = pltpu.CompilerParams(vmem_limit_bytes=V7X_VMEM_LIMIT_BYTES)
    tok = jax.ShapeDtypeStruct((nb, d), F32)
    tok16 = jax.ShapeDtypeStruct((nb, d), BF16)
    col = jax.ShapeDtypeStruct((d, nb), F32)
    ft, kt, qt, v, og, hb, ga, gb, h_new, c_new = pl.pallas_call(
        _sample_token_kernel,
        out_shape=(col, col, col, tok, tok, tok16, tok, tok, tok,
                   jax.ShapeDtypeStruct((CONV_W - 1, nb, d), F32)),
        name="sample_token", compiler_params=cp,
    )(x, h0, cbuf, p["g_mix_pre"], p["w_in"], p["lb_logits"], p["conv_w"], p["conv_b"],
      p["wa"], p["ba"], p["wx"], p["bx"], p["log_lambda"])

    s_new, o = pl.pallas_call(
        functools.partial(_sample_state_kernel, bb=bb), grid=(nb // bb,),
        in_specs=[pl.BlockSpec((bb, HEADS, HEAD_DIM, HEAD_DIM), lambda i: (i, 0, 0, 0)),
                  _full((d, nb)), _full((d, nb)), _full((d, nb)),
                  pl.BlockSpec((bb, d), lambda i: (i, 0))],
        out_specs=(pl.BlockSpec((bb, HEADS, HEAD_DIM, HEAD_DIM), lambda i: (i, 0, 0, 0)),
                   pl.BlockSpec((bb, d), lambda i: (i, 0))),
        out_shape=(jax.ShapeDtypeStruct(s0.shape, F32), tok),
        name="sample_state",
        compiler_params=pltpu.CompilerParams(
            dimension_semantics=("arbitrary",), vmem_limit_bytes=V7X_VMEM_LIMIT_BYTES),
    )(s0, ft, kt, qt, v)

    y = pl.pallas_call(
        _sample_tail_kernel, out_shape=tok, name="sample_tail", compiler_params=cp,
    )(x, o, og, hb, ga, gb, p["onorm"], p["g_mix_post"], p["w_pa"], p["w_pb"], p["w_o"],
      p["g_ffn_pre"], p["g_ffn_post"], p["w_fi"], p["w_fo"])
    return y, s_new, h_new, c_new


def kernel(x_prompt, x_sample, state_hgrn, state_lru, state_conv, norm_mix_pre, norm_mix_post,
           norm_ffn_pre, norm_ffn_post, w_in, hg_lb_logits, hg_out_norm, lru_conv_w, lru_conv_b,
           lru_w_a, lru_b_a, lru_w_x, lru_b_x, lru_log_lambda, w_branch_a, w_branch_b, w_out,
           w_ffn_in, w_ffn_out):
    depth = w_in.shape[0]
    assert depth == 1 and hg_lb_logits.shape[0] == 2
    nb, t, d = x_prompt.shape
    nbs = x_sample.shape[0]
    assert x_sample.shape[1] == 1

    def row(a):
        return a.reshape(1, -1).astype(F32)

    def chunks(w, cols=PROJ_COLS):
        return _split_cols(w.astype(BF16), cols)

    d_ff = w_ffn_out.shape[1]
    p = dict(
        g_mix_pre=row(norm_mix_pre[0]), g_mix_post=row(norm_mix_post[0]),
        g_ffn_pre=row(norm_ffn_pre[0]), g_ffn_post=row(norm_ffn_post[0]),
        w_in=chunks(w_in[0]), lb_logits=hg_lb_logits.astype(F32),
        onorm=row(hg_out_norm[0]), conv_w=lru_conv_w[0].astype(F32), conv_b=row(lru_conv_b[0]),
        wa=lru_w_a[0].astype(BF16), ba=row(lru_b_a[0]), wx=lru_w_x[0].astype(BF16),
        bx=row(lru_b_x[0]), log_lambda=row(lru_log_lambda[0]),
        w_pa=chunks(w_branch_a[0]), w_pb=chunks(w_branch_b[0]), w_o=chunks(w_out[0]),
        w_fi=jnp.concatenate([chunks(w_ffn_in[0][:, :d_ff], FFN_COLS),
                              chunks(w_ffn_in[0][:, d_ff:], FFN_COLS)], axis=0),
        w_fo=chunks(w_ffn_out[0]),
    )

    x1, s_p, h_p, c_p = _prompt_mixer(x_prompt, p, tm=256)
    y_p = _ffn_call(x1.reshape(nb * t, d), p, tm=256).reshape(nb, t, d)

    y_s, s_s, h_s, c_s = _sample_step(
        x_sample.reshape(nbs, d), state_hgrn[0], state_lru[0],
        jnp.swapaxes(state_conv[0], 0, 1), p)

    return (y_p, y_s.reshape(nbs, 1, d), s_p[None], h_p.reshape(1, nb, d), c_p[None],
            s_s[None], h_s[None], jnp.swapaxes(c_s, 0, 1)[None])
```

```python
import functools

import jax
import jax.numpy as jnp
from jax import lax
from jax.experimental import pallas as pl
from jax.experimental.pallas import tpu as pltpu

F32 = jnp.float32
BF16 = jnp.bfloat16

HEADS = 8
HEAD_DIM = 128
LRU_BLOCKS = 8
LRU_BLOCK_W = 128
CONV_W = 4
LRU_C = 8.0
EPS = 1e-6
CHUNK = 64
SUB = 16
SLAB = 8
PROJ_COLS = 512
FFN_COLS = 256
V7X_VMEM_LIMIT_BYTES = 56 * 1024 * 1024


def _rms(x, g):
    ms = jnp.mean(x * x, axis=-1, keepdims=True)
    return x * lax.rsqrt(ms + EPS) * g


def _bdot(a, w):
    return jnp.dot(a.astype(BF16), w, preferred_element_type=F32)


def _dot_nt(a, b):
    return lax.dot_general(a, b, (((1,), (1,)), ((), ())), preferred_element_type=F32)


def _dot_tn(a, b):
    return lax.dot_general(a, b, (((0,), (0,)), ((), ())), preferred_element_type=F32)


def _silu(x):
    return x * jax.nn.sigmoid(x)


def _gelu_tanh(x):
    c = 0.7978845608028654
    return x * (0.5 * (1.0 + jnp.tanh(c * (x + 0.044715 * (x * x * x)))))


def _softplus(z):
    return jnp.maximum(z, 0.0) + jnp.log1p(jnp.exp(-jnp.abs(z)))


def _forget_lower_bound(logits):
    m = jnp.max(logits, axis=0, keepdims=True)
    e = jnp.exp(logits - m)
    return e[0:1] / (e[0:1] + e[1:2])


def _lru_gates(xc, wa_ref, wx_ref, ba, bx, log_lambda):
    ra, ix = [], []
    for n in range(LRU_BLOCKS):
        sl = slice(n * LRU_BLOCK_W, (n + 1) * LRU_BLOCK_W)
        xb = xc[:, sl].astype(BF16)
        ra.append(jnp.dot(xb, wa_ref[n], preferred_element_type=F32))
        ix.append(jnp.dot(xb, wx_ref[n], preferred_element_type=F32))
    ra = jnp.concatenate(ra, axis=-1) + ba
    ix = jnp.concatenate(ix, axis=-1) + bx
    log_a = (-LRU_C) * jax.nn.sigmoid(ra) * _softplus(-log_lambda)
    a = jnp.exp(log_a)
    mult = jnp.sqrt(-jnp.tanh(log_a) * (a * a + 1.0))
    gate_in = jax.nn.sigmoid(ix) * xc
    return a, mult, gate_in


def _head_norm_gate(oh, og_silu, onorm):
    ms = jnp.mean(oh * oh, axis=-1, keepdims=True)
    return oh * lax.rsqrt(ms + EPS) * onorm * og_silu


def _col_refs(w_ref, cols):
    return [w_ref.at[:, c * cols:(c + 1) * cols] for c in range(w_ref.shape[1] // cols)]


def _col_specs(w, cols):
    n = w.shape[1] // cols
    assert n * cols == w.shape[1]
    specs = [pl.BlockSpec((w.shape[0], cols), functools.partial(lambda c, *_: (0, c), c),
                          pipeline_mode=pl.Buffered(1)) for c in range(n)]
    return specs, [w] * n


def _cdot(a, w_chunks):
    parts = [jnp.dot(a, w[...], preferred_element_type=F32) for w in w_chunks]
    return parts[0] if len(parts) == 1 else jnp.concatenate(parts, axis=-1)


def _merge_out(x, oa, hb, ga_sig, gb_sig, g_post, w_pa, w_pb, w_o):
    merged = ga_sig * _cdot(oa, w_pa) + gb_sig * _cdot(hb, w_pb)
    return x + _rms(_cdot(merged.astype(BF16), w_o), g_post)


def _ffn(x, g_pre, g_post, w_gate, w_up, w_fo):
    hn = _rms(x, g_pre).astype(BF16)
    act = jnp.concatenate(
        [(_silu(_cdot(hn, [wg])) * _cdot(hn, [wu])).astype(BF16)
         for wg, wu in zip(w_gate, w_up, strict=True)], axis=-1)
    return x + _rms(_cdot(act, w_fo), g_post)


def _chunk_prefix_sum(b2_ref, r0, sl, row8):
    slabs, carry = [], None
    for i in range(CHUNK // SLAB):
        vv = b2_ref[r0 + i * SLAB:r0 + (i + 1) * SLAB, sl]
        for s in (1, 2, 4):
            vv = vv + jnp.where(row8 >= s, pltpu.roll(vv, s, 0), 0.0)
        if carry is not None:
            vv = vv + carry
        carry = vv[SLAB - 1:SLAB]
        slabs.append(vv)
    return jnp.concatenate(slabs, axis=0)


def _prompt_mixer_kernel(*refs, tm):
    (x_ref, g_pre_ref, g_post_ref, lbl_ref, onorm_ref, cw_ref, cb_ref,
     wa_ref, ba_ref, wx_ref, bx_ref, ll_ref) = refs[:12]
    d = x_ref.shape[-1]
    per = d // PROJ_COLS
    weights = refs[12:12 + 11 * per]
    w_in, w_pa, w_pb, w_o = (weights[:8 * per], weights[8 * per:9 * per],
                             weights[9 * per:10 * per], weights[10 * per:])
    (y_ref, s_out_ref, h_out_ref, c_out_ref,
     st_ref, h_ref, xp_ref, xn_s, qh_s, b2_s, kk_s, v_s, og_s, ly_s, ga_s, gb_s, hg_s, hs_s,
     qin_s, a_s, u_s, dec_s, la_s, lb_s) = refs[12 + 11 * per:]
    j = pl.program_id(1)
    nchunk = tm // CHUNK
    nsub = CHUNK // SUB

    @pl.when(j == 0)
    def _():
        st_ref[...] = jnp.zeros_like(st_ref)
        h_ref[...] = jnp.zeros_like(h_ref)
        xp_ref[0:SLAB, :] = jnp.zeros((SLAB, d), F32)

    always = j >= 0

    def proj(g, c0):
        return jnp.dot(xn_s[...], w_in[(g * d + c0) // PROJ_COLS][...],
                       preferred_element_type=F32)

    def project_recurrence_inputs():
        xn_s[...] = _rms(x_ref[0], g_pre_ref[...]).astype(BF16)
        for c0 in range(0, d, PROJ_COLS):
            xp_ref[SLAB:SLAB + tm, c0:c0 + PROJ_COLS] = proj(4, c0)
        lb = _forget_lower_bound(lbl_ref[...])
        for c0 in range(0, d, PROJ_COLS):
            cs = slice(c0, c0 + PROJ_COLS)
            qh_s[:, cs] = _silu(proj(0, c0))
            f = lb[:, cs] + (1.0 - lb[:, cs]) * jax.nn.sigmoid(proj(1, c0))
            b2_s[:, cs] = jnp.log2(f)
            kk_s[:, cs] = 1.0 - f
            v_s[:, cs] = proj(2, c0).astype(BF16)

    def rglru_gates():
        cw = cw_ref[...]
        xc = cb_ref[...] + xp_ref[5:5 + tm, :] * cw[0:1]
        xc = xc + xp_ref[6:6 + tm, :] * cw[1:2]
        xc = xc + xp_ref[7:7 + tm, :] * cw[2:3]
        xc = xc + xp_ref[8:8 + tm, :] * cw[3:4]
        xp_ref[0:SLAB, :] = xp_ref[tm:tm + SLAB, :]
        a, mult, gate_in = _lru_gates(xc, wa_ref, wx_ref, ba_ref[...], bx_ref[...], ll_ref[...])
        row = lax.broadcasted_iota(jnp.int32, (tm, d), 0)
        mult = jnp.where(jnp.logical_and(row == 0, j == 0), 1.0, mult)
        la_s[...] = a
        lb_s[...] = mult * gate_in

    def project_gates():
        for c0 in range(0, d, PROJ_COLS):
            cs = slice(c0, c0 + PROJ_COLS)
            og_s[:, cs] = _silu(proj(3, c0))
            ly_s[:, cs] = _gelu_tanh(proj(5, c0))
            ga_s[:, cs] = jax.nn.sigmoid(proj(6, c0))
            gb_s[:, cs] = jax.nn.sigmoid(proj(7, c0))

    def hgrn2_scores():
        row8 = lax.broadcasted_iota(jnp.int32, (SLAB, HEAD_DIM), 0)
        for h in range(HEADS):
            sl = slice(h * HEAD_DIM, (h + 1) * HEAD_DIM)
            for c in range(nchunk):
                r0 = c * CHUNK
                rs = slice(r0, r0 + CHUNK)
                bc = _chunk_prefix_sum(b2_s, r0, sl, row8)
                qc, kc, vc = qh_s[rs, sl], kk_s[rs, sl], v_s[rs, sl]
                b_last = bc[CHUNK - 1:CHUNK]
                qin_s[rs, sl] = (qc * jnp.exp2(bc)).astype(BF16)
                dec_s[c:c + 1, sl] = jnp.exp2(b_last)
                for ib in range(nsub):
                    t0, t1 = ib * SUB, (ib + 1) * SUB
                    ref_b = bc[t0 + SUB // 2 - 1:t0 + SUB // 2]
                    qs = (qc[t0:t1] * jnp.exp2(bc[t0:t1] - ref_b)).astype(BF16)
                    ks = (kc[:t1] * jnp.exp2(ref_b - bc[:t1])).astype(BF16)
                    a_blk = _dot_nt(qs, ks)
                    ti = lax.broadcasted_iota(jnp.int32, a_blk.shape, 0) + t0
                    si = lax.broadcasted_iota(jnp.int32, a_blk.shape, 1)
                    a_s[r0 + t0:r0 + t1, h * HEAD_DIM:h * HEAD_DIM + t1] = (
                        jnp.where(ti >= si, a_blk, 0.0).astype(BF16))
                kst = (kc * jnp.exp2(b_last - bc)).astype(BF16)
                u_s[h * nchunk + c] = _dot_tn(vc, kst)

    def hgrn2_recurrence():
        for h in range(HEADS):
            sl = slice(h * HEAD_DIM, (h + 1) * HEAD_DIM)
            st = st_ref[h]
            for c in range(nchunk):
                r0 = c * CHUNK
                rs = slice(r0, r0 + CHUNK)
                o_c = _dot_nt(qin_s[rs, sl], st.astype(BF16))
                rows = []
                for ib in range(nsub):
                    t0, t1 = ib * SUB, (ib + 1) * SUB
                    a_blk = a_s[r0 + t0:r0 + t1, h * HEAD_DIM:h * HEAD_DIM + t1]
                    rows.append(o_c[t0:t1] + jnp.dot(a_blk, v_s[r0:r0 + t1, sl],
                                                     preferred_element_type=F32))
                hg_s[rs, sl] = jnp.concatenate(rows, axis=0)
                st = dec_s[c:c + 1, sl] * st + u_s[h * nchunk + c]
            st_ref[h] = st

    def rglru_scan():
        row8d = lax.broadcasted_iota(jnp.int32, (SLAB, d), 0)
        hc = h_ref[...]
        for i in range(tm // SLAB):
            a8 = la_s[i * SLAB:(i + 1) * SLAB, :]
            b8 = lb_s[i * SLAB:(i + 1) * SLAB, :]
            for s in (1, 2, 4):
                m = row8d >= s
                b8 = b8 + a8 * jnp.where(m, pltpu.roll(b8, s, 0), 0.0)
                a8 = a8 * jnp.where(m, pltpu.roll(a8, s, 0), 1.0)
            h8 = a8 * hc + b8
            hc = h8[SLAB - 1:SLAB]
            hs_s[i * SLAB:(i + 1) * SLAB, :] = h8
        h_ref[...] = hc

    def merge_out():
        onorm = onorm_ref[...]
        oa = jnp.concatenate(
            [_head_norm_gate(hg_s[:, h * HEAD_DIM:(h + 1) * HEAD_DIM],
                             og_s[:, h * HEAD_DIM:(h + 1) * HEAD_DIM], onorm)
             for h in range(HEADS)], axis=-1).astype(BF16)
        hb = (ly_s[...] * hs_s[...]).astype(BF16)
        y_ref[0] = _merge_out(x_ref[0], oa, hb, ga_s[...], gb_s[...], g_post_ref[...],
                              w_pa, w_pb, w_o)

    @pl.when(always)
    def _():
        project_recurrence_inputs()
        rglru_gates()

    @pl.when(always)
    def _():
        hgrn2_scores()
        project_gates()

    @pl.when(always)
    def _():
        hgrn2_recurrence()
        rglru_scan()

    @pl.when(always)
    def _():
        merge_out()

    @pl.when(j == pl.num_programs(1) - 1)
    def _():
        for h in range(HEADS):
            s_out_ref[0, h] = st_ref[h].T
        h_out_ref[0] = h_ref[...]
        c_out_ref[0] = xp_ref[5:8, :]


def _full(shape):
    return pl.BlockSpec(shape, lambda *_: (0,) * len(shape))


def _prompt_mixer(x, p, tm):
    nb, t, d = x.shape
    assert t % tm == 0 and tm % CHUNK == 0 and d == HEADS * HEAD_DIM and d % PROJ_COLS == 0
    assert tm // CHUNK <= SLAB
    kern = functools.partial(_prompt_mixer_kernel, tm=tm)
    in_specs = [
        pl.BlockSpec((1, tm, d), lambda s, j: (s, j, 0)),
        _full((1, d)), _full((1, d)), _full(p["lb_logits"].shape),
        _full((1, HEAD_DIM)), _full((CONV_W, d)), _full((1, d)),
        _full(p["wa"].shape), _full((1, d)), _full(p["wx"].shape), _full((1, d)), _full((1, d)),
    ]
    weights = []
    for name in ("w_in", "w_pa", "w_pb", "w_o"):
        specs, ops = _col_specs(p[name], PROJ_COLS)
        in_specs += specs
        weights += ops
    out_shape = (
        jax.ShapeDtypeStruct((nb, t, d), F32),
        jax.ShapeDtypeStruct((nb, HEADS, HEAD_DIM, HEAD_DIM), F32),
        jax.ShapeDtypeStruct((nb, 1, d), F32),
        jax.ShapeDtypeStruct((nb, CONV_W - 1, d), F32),
    )
    out_specs = (
        pl.BlockSpec((1, tm, d), lambda s, j: (s, j, 0)),
        pl.BlockSpec((1, HEADS, HEAD_DIM, HEAD_DIM), lambda s, j: (s, 0, 0, 0)),
        pl.BlockSpec((1, 1, d), lambda s, j: (s, 0, 0)),
        pl.BlockSpec((1, CONV_W - 1, d), lambda s, j: (s, 0, 0)),
    )
    tile_f32 = pltpu.VMEM((tm, d), F32)
    tile_bf16 = pltpu.VMEM((tm, d), BF16)
    scratch = [
        pltpu.VMEM((HEADS, HEAD_DIM, HEAD_DIM), F32),
        pltpu.VMEM((1, d), F32),
        pltpu.VMEM((tm + SLAB, d), F32),
        tile_bf16,
        tile_f32, tile_f32, tile_f32,
        tile_bf16,
        tile_f32, tile_f32, tile_f32, tile_f32,
        tile_f32,
        tile_f32,
        tile_bf16,
        tile_bf16,
        pltpu.VMEM((HEADS * (tm // CHUNK), HEAD_DIM, HEAD_DIM), F32),
        pltpu.VMEM((SLAB, d), F32),
        tile_f32, tile_f32,
    ]
    return pl.pallas_call(
        kern, grid=(nb, t // tm), in_specs=in_specs, out_specs=out_specs, out_shape=out_shape,
        scratch_shapes=scratch, name="prompt_mixer",
        compiler_params=pltpu.CompilerParams(
            dimension_semantics=("arbitrary", "arbitrary"),
            vmem_limit_bytes=V7X_VMEM_LIMIT_BYTES),
    )(x, p["g_mix_pre"], p["g_mix_post"], p["lb_logits"], p["onorm"], p["conv_w"],
      p["conv_b"], p["wa"], p["ba"], p["wx"], p["bx"], p["log_lambda"], *weights)


def _ffn_kernel(*refs, nf):
    x_ref, g_pre_ref, g_post_ref = refs[:3]
    w_gate, w_up, w_fo = refs[3:3 + nf], refs[3 + nf:3 + 2 * nf], refs[3 + 2 * nf:-1]
    refs[-1][...] = _ffn(x_ref[...], g_pre_ref[...], g_post_ref[...], w_gate, w_up, w_fo)


def _ffn_call(x, p, tm):
    n, d = x.shape
    assert n % tm == 0
    fi_specs, fi_ops = _col_specs(p["w_fi"], FFN_COLS)
    fo_specs, fo_ops = _col_specs(p["w_fo"], PROJ_COLS)
    return pl.pallas_call(
        functools.partial(_ffn_kernel, nf=len(fi_ops) // 2), grid=(n // tm,),
        in_specs=[pl.BlockSpec((tm, d), lambda i: (i, 0)), _full((1, d)), _full((1, d))]
        + fi_specs + fo_specs,
        out_specs=pl.BlockSpec((tm, d), lambda i: (i, 0)),
        out_shape=jax.ShapeDtypeStruct((n, d), F32), name="swiglu",
        compiler_params=pltpu.CompilerParams(
            dimension_semantics=("arbitrary",), vmem_limit_bytes=V7X_VMEM_LIMIT_BYTES),
    )(x, p["g_ffn_pre"], p["g_ffn_post"], *fi_ops, *fo_ops)


def _sample_token_kernel(
        x_ref, h0_ref, cbuf_ref, g_pre_ref, w_in_ref, lbl_ref, cw_ref, cb_ref,
        wa_ref, ba_ref, wx_ref, bx_ref, ll_ref,
        ft_ref, kt_ref, qt_ref, v_ref, og_ref, hb_ref, ga_ref, gb_ref, h_out_ref, c_out_ref):
    d = x_ref.shape[-1]
    xn = _rms(x_ref[...], g_pre_ref[...]).astype(BF16)

    def proj(i):
        return _cdot(xn, [w_in_ref.at[:, i * d:(i + 1) * d]])

    lb = _forget_lower_bound(lbl_ref[...])
    qh = _silu(proj(0))
    f = lb + (1.0 - lb) * jax.nn.sigmoid(proj(1))
    kk = 1.0 - f
    for h in range(HEADS):
        sl = slice(h * HEAD_DIM, (h + 1) * HEAD_DIM)
        ft_ref[sl, :] = f[:, sl].T
        kt_ref[sl, :] = kk[:, sl].T
        qt_ref[sl, :] = qh[:, sl].T
    v_ref[...] = proj(2)
    og_ref[...] = _silu(proj(3))

    lx = proj(4)
    cw = cw_ref[...]
    xc = cb_ref[...] + cbuf_ref[0] * cw[0:1]
    xc = xc + cbuf_ref[1] * cw[1:2]
    xc = xc + cbuf_ref[2] * cw[2:3]
    xc = xc + lx * cw[3:4]
    c_out_ref[0] = cbuf_ref[1]
    c_out_ref[1] = cbuf_ref[2]
    c_out_ref[2] = lx
    a, mult, gate_in = _lru_gates(xc, wa_ref, wx_ref, ba_ref[...], bx_ref[...], ll_ref[...])
    hnew = a * h0_ref[...] + mult * gate_in
    h_out_ref[...] = hnew
    hb_ref[...] = (_gelu_tanh(proj(5)) * hnew).astype(BF16)
    ga_ref[...] = jax.nn.sigmoid(proj(6))
    gb_ref[...] = jax.nn.sigmoid(proj(7))


def _sample_state_kernel(s_ref, ft_ref, kt_ref, qt_ref, v_ref, s_out_ref, o_ref, *, bb):
    i = pl.program_id(0)
    lanes = ft_ref.shape[-1]
    shift = lax.rem(lanes - i * bb, lanes)
    ft = pltpu.roll(ft_ref[...], shift, 1)
    kt = pltpu.roll(kt_ref[...], shift, 1)
    qt = pltpu.roll(qt_ref[...], shift, 1)
    for u in range(bb):
        for h in range(HEADS):
            sl = slice(h * HEAD_DIM, (h + 1) * HEAD_DIM)
            s_new = ft[sl, u:u + 1] * s_ref[u, h] + kt[sl, u:u + 1] * v_ref[u:u + 1, sl]
            s_out_ref[u, h] = s_new
            o_ref[u:u + 1, sl] = jnp.sum(qt[sl, u:u + 1] * s_new, axis=0, keepdims=True)


def _sample_tail_kernel(
        x_ref, o_ref, og_ref, hb_ref, ga_ref, gb_ref, onorm_ref, g_post_ref, w_pa_ref, w_pb_ref,
        w_o_ref, g_fpre_ref, g_fpost_ref, w_fi_ref, w_fo_ref, y_ref):
    o = o_ref[...]
    og = og_ref[...]
    oa = jnp.concatenate(
        [_head_norm_gate(o[:, h * HEAD_DIM:(h + 1) * HEAD_DIM],
                         og[:, h * HEAD_DIM:(h + 1) * HEAD_DIM], onorm_ref[...])
         for h in range(HEADS)], axis=-1).astype(BF16)
    x1 = _merge_out(x_ref[...], oa, hb_ref[...], ga_ref[...], gb_ref[...], g_post_ref[...],
                    [w_pa_ref], [w_pb_ref], [w_o_ref])
    d_ff = w_fo_ref.shape[0]
    w_fi = _col_refs(w_fi_ref, d_ff)
    y_ref[...] = _ffn(x1, g_fpre_ref[...], g_fpost_ref[...], w_fi[:1], w_fi[1:], [w_fo_ref])


def _sample_step(x, s0, h0, cbuf, p, bb=8):
    nb, d = x.shape
    assert nb == HEAD_DIM and nb % bb == 0
    cp = pltpu.CompilerParams(vmem_limit_bytes=V7X_VMEM_LIMIT_BYTES)
    tok = jax.ShapeDtypeStruct((nb, d), F32)
    tok16 = jax.ShapeDtypeStruct((nb, d), BF16)
    col = jax.ShapeDtypeStruct((d, nb), F32)
    ft, kt, qt, v, og, hb, ga, gb, h_new, c_new = pl.pallas_call(
        _sample_token_kernel,
        out_shape=(col, col, col, tok, tok, tok16, tok, tok, tok,
                   jax.ShapeDtypeStruct((CONV_W - 1, nb, d), F32)),
        name="sample_token", compiler_params=cp,
    )(x, h0, cbuf, p["g_mix_pre"], p["w_in"], p["lb_logits"], p["conv_w"], p["conv_b"],
      p["wa"], p["ba"], p["wx"], p["bx"], p["log_lambda"])

    s_new, o = pl.pallas_call(
        functools.partial(_sample_state_kernel, bb=bb), grid=(nb // bb,),
        in_specs=[pl.BlockSpec((bb, HEADS, HEAD_DIM, HEAD_DIM), lambda i: (i, 0, 0, 0)),
                  _full((d, nb)), _full((d, nb)), _full((d, nb)),
                  pl.BlockSpec((bb, d), lambda i: (i, 0))],
        out_specs=(pl.BlockSpec((bb, HEADS, HEAD_DIM, HEAD_DIM), lambda i: (i, 0, 0, 0)),
                   pl.BlockSpec((bb, d), lambda i: (i, 0))),
        out_shape=(jax.ShapeDtypeStruct(s0.shape, F32), tok),
        name="sample_state",
        compiler_params=pltpu.CompilerParams(
            dimension_semantics=("arbitrary",), vmem_limit_bytes=V7X_VMEM_LIMIT_BYTES),
    )(s0, ft, kt, qt, v)

    y = pl.pallas_call(
        _sample_tail_kernel, out_shape=tok, name="sample_tail", compiler_params=cp,
    )(x, o, og, hb, ga, gb, p["onorm"], p["g_mix_post"], p["w_pa"], p["w_pb"], p["w_o"],
      p["g_ffn_pre"], p["g_ffn_post"], p["w_fi"], p["w_fo"])
    return y, s_new, h_new, c_new


def kernel(x_prompt, x_sample, state_hgrn, state_lru, state_conv, norm_mix_pre, norm_mix_post,
           norm_ffn_pre, norm_ffn_post, w_in, hg_lb_logits, hg_out_norm, lru_conv_w, lru_conv_b,
           lru_w_a, lru_b_a, lru_w_x, lru_b_x, lru_log_lambda, w_branch_a, w_branch_b, w_out,
           w_ffn_in, w_ffn_out):
    depth = w_in.shape[0]
    assert depth == 1 and hg_lb_logits.shape[0] == 2
    nb, t, d = x_prompt.shape
    nbs = x_sample.shape[0]
    assert x_sample.shape[1] == 1

    def row(a):
        return a.reshape(1, -1).astype(F32)

    p = dict(
        g_mix_pre=row(norm_mix_pre[0]), g_mix_post=row(norm_mix_post[0]),
        g_ffn_pre=row(norm_ffn_pre[0]), g_ffn_post=row(norm_ffn_post[0]),
        w_in=w_in[0].astype(BF16), lb_logits=hg_lb_logits.astype(F32),
        onorm=row(hg_out_norm[0]), conv_w=lru_conv_w[0].astype(F32), conv_b=row(lru_conv_b[0]),
        wa=lru_w_a[0].astype(BF16), ba=row(lru_b_a[0]), wx=lru_w_x[0].astype(BF16),
        bx=row(lru_b_x[0]), log_lambda=row(lru_log_lambda[0]),
        w_pa=w_branch_a[0].astype(BF16), w_pb=w_branch_b[0].astype(BF16),
        w_o=w_out[0].astype(BF16), w_fi=w_ffn_in[0].astype(BF16), w_fo=w_ffn_out[0].astype(BF16),
    )

    x1, s_p, h_p, c_p = _prompt_mixer(x_prompt, p, tm=256)
    y_p = _ffn_call(x1.reshape(nb * t, d), p, tm=256).reshape(nb, t, d)

    y_s, s_s, h_s, c_s = _sample_step(
        x_sample.reshape(nbs, d), state_hgrn[0], state_lru[0],
        jnp.swapaxes(state_conv[0], 0, 1), p)

    return (y_p, y_s.reshape(nbs, 1, d), s_p[None], h_p.reshape(1, nb, d), c_p[None],
            s_s[None], h_s[None], jnp.swapaxes(c_s, 0, 1)[None])
```

```python
import functools

import jax
import jax.numpy as jnp
from jax import lax
from jax.experimental import pallas as pl
from jax.experimental.pallas import tpu as pltpu

F32 = jnp.float32
BF16 = jnp.bfloat16

HEADS = 8
HEAD_DIM = 128
LRU_BLOCKS = 8
LRU_BLOCK_W = 128
CONV_W = 4
LRU_C = 8.0
EPS = 1e-6
CHUNK = 64
SUB = 16
SLAB = 8
PROJ_COLS = 512
FFN_COLS = 256
V7X_VMEM_LIMIT_BYTES = 56 * 1024 * 1024


def _rms(x, g):
    ms = jnp.mean(x * x, axis=-1, keepdims=True)
    return x * lax.rsqrt(ms + EPS) * g


def _bdot(a, w):
    return jnp.dot(a.astype(BF16), w, preferred_element_type=F32)


def _dot_nt(a, b):
    return lax.dot_general(a, b, (((1,), (1,)), ((), ())), preferred_element_type=F32)


def _dot_tn(a, b):
    return lax.dot_general(a, b, (((0,), (0,)), ((), ())), preferred_element_type=F32)


def _silu(x):
    return x * jax.nn.sigmoid(x)


def _gelu_tanh(x):
    c = 0.7978845608028654
    return x * (0.5 * (1.0 + jnp.tanh(c * (x + 0.044715 * (x * x * x)))))


def _softplus(z):
    return jnp.maximum(z, 0.0) + jnp.log1p(jnp.exp(-jnp.abs(z)))


def _forget_lower_bound(logits):
    m = jnp.max(logits, axis=0, keepdims=True)
    e = jnp.exp(logits - m)
    return e[0:1] / (e[0:1] + e[1:2])


def _lru_gates(xc, wa_ref, wx_ref, ba, bx, log_lambda):
    ra, ix = [], []
    for n in range(LRU_BLOCKS):
        sl = slice(n * LRU_BLOCK_W, (n + 1) * LRU_BLOCK_W)
        xb = xc[:, sl].astype(BF16)
        ra.append(jnp.dot(xb, wa_ref[n], preferred_element_type=F32))
        ix.append(jnp.dot(xb, wx_ref[n], preferred_element_type=F32))
    ra = jnp.concatenate(ra, axis=-1) + ba
    ix = jnp.concatenate(ix, axis=-1) + bx
    log_a = (-LRU_C) * jax.nn.sigmoid(ra) * _softplus(-log_lambda)
    a = jnp.exp(log_a)
    mult = jnp.sqrt(-jnp.tanh(log_a) * (a * a + 1.0))
    gate_in = jax.nn.sigmoid(ix) * xc
    return a, mult, gate_in


def _head_norm_gate(oh, og_silu, onorm):
    ms = jnp.mean(oh * oh, axis=-1, keepdims=True)
    return oh * lax.rsqrt(ms + EPS) * onorm * og_silu


def _col_refs(w_ref, cols):
    return [w_ref.at[:, c * cols:(c + 1) * cols] for c in range(w_ref.shape[1] // cols)]


def _col_specs(w, cols):
    n = w.shape[1] // cols
    assert n * cols == w.shape[1]
    specs = [pl.BlockSpec((w.shape[0], cols), functools.partial(lambda c, *_: (0, c), c),
                          pipeline_mode=pl.Buffered(1)) for c in range(n)]
    return specs, [w] * n


def _cdot(a, w_chunks):
    parts = [jnp.dot(a, w[...], preferred_element_type=F32) for w in w_chunks]
    return parts[0] if len(parts) == 1 else jnp.concatenate(parts, axis=-1)


def _merge_out(x, oa, hb, ga_sig, gb_sig, g_post, w_pa, w_pb, w_o):
    merged = ga_sig * _cdot(oa, w_pa) + gb_sig * _cdot(hb, w_pb)
    return x + _rms(_cdot(merged.astype(BF16), w_o), g_post)


def _ffn(x, g_pre, g_post, w_gate, w_up, w_fo):
    hn = _rms(x, g_pre).astype(BF16)
    act = jnp.concatenate(
        [(_silu(_cdot(hn, [wg])) * _cdot(hn, [wu])).astype(BF16)
         for wg, wu in zip(w_gate, w_up, strict=True)], axis=-1)
    return x + _rms(_cdot(act, w_fo), g_post)


def _chunk_prefix_sum(b2_ref, r0, sl, row8):
    slabs, carry = [], None
    for i in range(CHUNK // SLAB):
        vv = b2_ref[r0 + i * SLAB:r0 + (i + 1) * SLAB, sl]
        for s in (1, 2, 4):
            vv = vv + jnp.where(row8 >= s, pltpu.roll(vv, s, 0), 0.0)
        if carry is not None:
            vv = vv + carry
        carry = vv[SLAB - 1:SLAB]
        slabs.append(vv)
    return jnp.concatenate(slabs, axis=0)


def _prompt_mixer_kernel(*refs, tm):
    (x_ref, g_pre_ref, g_post_ref, lbl_ref, onorm_ref, cw_ref, cb_ref,
     wa_ref, ba_ref, wx_ref, bx_ref, ll_ref) = refs[:12]
    d = x_ref.shape[-1]
    per = d // PROJ_COLS
    weights = refs[12:12 + 11 * per]
    w_in, w_pa, w_pb, w_o = (weights[:8 * per], weights[8 * per:9 * per],
                             weights[9 * per:10 * per], weights[10 * per:])
    (y_ref, s_out_ref, h_out_ref, c_out_ref,
     st_ref, h_ref, xp_ref, xn_s, qh_s, b2_s, kk_s, v_s, og_s, ly_s, ga_s, gb_s, hg_s, hs_s,
     qin_s, a_s, u_s, dec_s, la_s, lb_s) = refs[12 + 11 * per:]
    j = pl.program_id(1)
    nchunk = tm // CHUNK
    nsub = CHUNK // SUB

    @pl.when(j == 0)
    def _():
        st_ref[...] = jnp.zeros_like(st_ref)
        h_ref[...] = jnp.zeros_like(h_ref)
        xp_ref[0:SLAB, :] = jnp.zeros((SLAB, d), F32)

    always = j >= 0

    def proj(g, c0):
        return jnp.dot(xn_s[...], w_in[(g * d + c0) // PROJ_COLS][...],
                       preferred_element_type=F32)

    def project_recurrence_inputs():
        xn_s[...] = _rms(x_ref[0], g_pre_ref[...]).astype(BF16)
        for c0 in range(0, d, PROJ_COLS):
            xp_ref[SLAB:SLAB + tm, c0:c0 + PROJ_COLS] = proj(4, c0)
        lb = _forget_lower_bound(lbl_ref[...])
        for c0 in range(0, d, PROJ_COLS):
            cs = slice(c0, c0 + PROJ_COLS)
            qh_s[:, cs] = _silu(proj(0, c0))
            f = lb[:, cs] + (1.0 - lb[:, cs]) * jax.nn.sigmoid(proj(1, c0))
            b2_s[:, cs] = jnp.log2(f)
            kk_s[:, cs] = 1.0 - f
            v_s[:, cs] = proj(2, c0).astype(BF16)

    def rglru_gates():
        cw = cw_ref[...]
        xc = cb_ref[...] + xp_ref[5:5 + tm, :] * cw[0:1]
        xc = xc + xp_ref[6:6 + tm, :] * cw[1:2]
        xc = xc + xp_ref[7:7 + tm, :] * cw[2:3]
        xc = xc + xp_ref[8:8 + tm, :] * cw[3:4]
        xp_ref[0:SLAB, :] = xp_ref[tm:tm + SLAB, :]
        a, mult, gate_in = _lru_gates(xc, wa_ref, wx_ref, ba_ref[...], bx_ref[...], ll_ref[...])
        row = lax.broadcasted_iota(jnp.int32, (tm, d), 0)
        mult = jnp.where(jnp.logical_and(row == 0, j == 0), 1.0, mult)
        la_s[...] = a
        lb_s[...] = mult * gate_in

    def project_gates():
        for c0 in range(0, d, PROJ_COLS):
            cs = slice(c0, c0 + PROJ_COLS)
            og_s[:, cs] = _silu(proj(3, c0))
            ly_s[:, cs] = _gelu_tanh(proj(5, c0))
            ga_s[:, cs] = jax.nn.sigmoid(proj(6, c0))
            gb_s[:, cs] = jax.nn.sigmoid(proj(7, c0))

    def hgrn2_scores():
        row8 = lax.broadcasted_iota(jnp.int32, (SLAB, HEAD_DIM), 0)
        for h in range(HEADS):
            sl = slice(h * HEAD_DIM, (h + 1) * HEAD_DIM)
            for c in range(nchunk):
                r0 = c * CHUNK
                rs = slice(r0, r0 + CHUNK)
                bc = _chunk_prefix_sum(b2_s, r0, sl, row8)
                qc, kc, vc = qh_s[rs, sl], kk_s[rs, sl], v_s[rs, sl]
                b_last = bc[CHUNK - 1:CHUNK]
                qin_s[rs, sl] = (qc * jnp.exp2(bc)).astype(BF16)
                dec_s[c:c + 1, sl] = jnp.exp2(b_last)
                for ib in range(nsub):
                    t0, t1 = ib * SUB, (ib + 1) * SUB
                    ref_b = bc[t0 + SUB // 2 - 1:t0 + SUB // 2]
                    qs = (qc[t0:t1] * jnp.exp2(bc[t0:t1] - ref_b)).astype(BF16)
                    ks = (kc[:t1] * jnp.exp2(ref_b - bc[:t1])).astype(BF16)
                    a_blk = _dot_nt(qs, ks)
                    ti = lax.broadcasted_iota(jnp.int32, a_blk.shape, 0) + t0
                    si = lax.broadcasted_iota(jnp.int32, a_blk.shape, 1)
                    a_s[r0 + t0:r0 + t1, h * HEAD_DIM:h * HEAD_DIM + t1] = (
                        jnp.where(ti >= si, a_blk, 0.0).astype(BF16))
                kst = (kc * jnp.exp2(b_last - bc)).astype(BF16)
                u_s[h * nchunk + c] = _dot_tn(vc, kst)

    def hgrn2_recurrence():
        for h in range(HEADS):
            sl = slice(h * HEAD_DIM, (h + 1) * HEAD_DIM)
            st = st_ref[h]
            for c in range(nchunk):
                r0 = c * CHUNK
                rs = slice(r0, r0 + CHUNK)
                o_c = _dot_nt(qin_s[rs, sl], st.astype(BF16))
                rows = []
                for ib in range(nsub):
                    t0, t1 = ib * SUB, (ib + 1) * SUB
                    a_blk = a_s[r0 + t0:r0 + t1, h * HEAD_DIM:h * HEAD_DIM + t1]
                    rows.append(o_c[t0:t1] + jnp.dot(a_blk, v_s[r0:r0 + t1, sl],
                                                     preferred_element_type=F32))
                hg_s[rs, sl] = jnp.concatenate(rows, axis=0)
                st = dec_s[c:c + 1, sl] * st + u_s[h * nchunk + c]
            st_ref[h] = st

    def rglru_scan():
        row8d = lax.broadcasted_iota(jnp.int32, (SLAB, d), 0)
        hc = h_ref[...]
        for i in range(tm // SLAB):
            a8 = la_s[i * SLAB:(i + 1) * SLAB, :]
            b8 = lb_s[i * SLAB:(i + 1) * SLAB, :]
            for s in (1, 2, 4):
                m = row8d >= s
                b8 = b8 + a8 * jnp.where(m, pltpu.roll(b8, s, 0), 0.0)
                a8 = a8 * jnp.where(m, pltpu.roll(a8, s, 0), 1.0)
            h8 = a8 * hc + b8
            hc = h8[SLAB - 1:SLAB]
            hs_s[i * SLAB:(i + 1) * SLAB, :] = h8
        h_ref[...] = hc

    def merge_out():
        onorm = onorm_ref[...]
        oa = jnp.concatenate(
            [_head_norm_gate(hg_s[:, h * HEAD_DIM:(h + 1) * HEAD_DIM],
                             og_s[:, h * HEAD_DIM:(h + 1) * HEAD_DIM], onorm)
             for h in range(HEADS)], axis=-1).astype(BF16)
        hb = (ly_s[...] * hs_s[...]).astype(BF16)
        y_ref[0] = _merge_out(x_ref[0], oa, hb, ga_s[...], gb_s[...], g_post_ref[...],
                              w_pa, w_pb, w_o)

    @pl.when(always)
    def _():
        project_recurrence_inputs()
        rglru_gates()

    @pl.when(always)
    def _():
        hgrn2_scores()
        project_gates()

    @pl.when(always)
    def _():
        hgrn2_recurrence()
        rglru_scan()

    @pl.when(always)
    def _():
        merge_out()

    @pl.when(j == pl.num_programs(1) - 1)
    def _():
        for h in range(HEADS):
            s_out_ref[0, h] = st_ref[h].T
        h_out_ref[0] = h_ref[...]
        c_out_ref[0] = xp_ref[5:8, :]


def _full(shape):
    return pl.BlockSpec(shape, lambda *_: (0,) * len(shape))


def _prompt_mixer(x, p, tm):
    nb, t, d = x.shape
    assert t % tm == 0 and tm % CHUNK == 0 and d == HEADS * HEAD_DIM and d % PROJ_COLS == 0
    assert tm // CHUNK <= SLAB
    kern = functools.partial(_prompt_mixer_kernel, tm=tm)
    in_specs = [
        pl.BlockSpec((1, tm, d), lambda s, j: (s, j, 0)),
        _full((1, d)), _full((1, d)), _full(p["lb_logits"].shape),
        _full((1, HEAD_DIM)), _full((CONV_W, d)), _full((1, d)),
        _full(p["wa"].shape), _full((1, d)), _full(p["wx"].shape), _full((1, d)), _full((1, d)),
    ]
    weights = []
    for name in ("w_in", "w_pa", "w_pb", "w_o"):
        specs, ops = _col_specs(p[name], PROJ_COLS)
        in_specs += specs
        weights += ops
    out_shape = (
        jax.ShapeDtypeStruct((nb, t, d), F32),
        jax.ShapeDtypeStruct((nb, HEADS, HEAD_DIM, HEAD_DIM), F32),
        jax.ShapeDtypeStruct((nb, 1, d), F32),
        jax.ShapeDtypeStruct((nb, CONV_W - 1, d), F32),
    )
    out_specs = (
        pl.BlockSpec((1, tm, d), lambda s, j: (s, j, 0)),
        pl.BlockSpec((1, HEADS, HEAD_DIM, HEAD_DIM), lambda s, j: (s, 0, 0, 0)),
        pl.BlockSpec((1, 1, d), lambda s, j: (s, 0, 0)),
        pl.BlockSpec((1, CONV_W - 1, d), lambda s, j: (s, 0, 0)),
    )
    tile_f32 = pltpu.VMEM((tm, d), F32)
    tile_bf16 = pltpu.VMEM((tm, d), BF16)
    scratch = [
        pltpu.VMEM((HEADS, HEAD_DIM, HEAD_DIM), F32),
        pltpu.VMEM((1, d), F32),
        pltpu.VMEM((tm + SLAB, d), F32),
        tile_bf16,
        tile_f32, tile_f32, tile_f32,
        tile_bf16,
        tile_f32, tile_f32, tile_f32, tile_f32,
        tile_f32,
        tile_f32,
        tile_bf16,
        tile_bf16,
        pltpu.VMEM((HEADS * (tm // CHUNK), HEAD_DIM, HEAD_DIM), F32),
        pltpu.VMEM((SLAB, d), F32),
        tile_f32, tile_f32,
    ]
    return pl.pallas_call(
        kern, grid=(nb, t // tm), in_specs=in_specs, out_specs=out_specs, out_shape=out_shape,
        scratch_shapes=scratch, name="prompt_mixer",
        compiler_params=pltpu.CompilerParams(
            dimension_semantics=("arbitrary", "arbitrary"),
            vmem_limit_bytes=V7X_VMEM_LIMIT_BYTES),
    )(x, p["g_mix_pre"], p["g_mix_post"], p["lb_logits"], p["onorm"], p["conv_w"],
      p["conv_b"], p["wa"], p["ba"], p["wx"], p["bx"], p["log_lambda"], *weights)


def _ffn_kernel(*refs, nf):
    x_ref, g_pre_ref, g_post_ref = refs[:3]
    w_gate, w_up, w_fo = refs[3:3 + nf], refs[3 + nf:3 + 2 * nf], refs[3 + 2 * nf:-1]
    refs[-1][...] = _ffn(x_ref[...], g_pre_ref[...], g_post_ref[...], w_gate, w_up, w_fo)


def _ffn_call(x, p, tm):
    n, d = x.shape
    assert n % tm == 0
    fi_specs, fi_ops = _col_specs(p["w_fi"], FFN_COLS)
    fo_specs, fo_ops = _col_specs(p["w_fo"], PROJ_COLS)
    return pl.pallas_call(
        functools.partial(_ffn_kernel, nf=len(fi_ops) // 2), grid=(n // tm,),
        in_specs=[pl.BlockSpec((tm, d), lambda i: (i, 0)), _full((1, d)), _full((1, d))]
        + fi_specs + fo_specs,
        out_specs=pl.BlockSpec((tm, d), lambda i: (i, 0)),
        out_shape=jax.ShapeDtypeStruct((n, d), F32), name="swiglu",
        compiler_params=pltpu.CompilerParams(
            dimension_semantics=("arbitrary",), vmem_limit_bytes=V7X_VMEM_LIMIT_BYTES),
    )(x, p["g_ffn_pre"], p["g_ffn_post"], *fi_ops, *fo_ops)


def _sample_token_kernel(
        x_ref, h0_ref, cbuf_ref, g_pre_ref, w_in_ref, lbl_ref, cw_ref, cb_ref,
        wa_ref, ba_ref, wx_ref, bx_ref, ll_ref,
        ft_ref, kt_ref, q_ref, v_ref, og_ref, hb_ref, ga_ref, gb_ref, h_out_ref, c_out_ref):
    d = x_ref.shape[-1]
    xn = _rms(x_ref[...], g_pre_ref[...]).astype(BF16)

    def proj(i):
        return _cdot(xn, [w_in_ref.at[:, i * d:(i + 1) * d]])

    lb = _forget_lower_bound(lbl_ref[...])
    qh = _silu(proj(0))
    f = lb + (1.0 - lb) * jax.nn.sigmoid(proj(1))
    kk = 1.0 - f
    for h in range(HEADS):
        sl = slice(h * HEAD_DIM, (h + 1) * HEAD_DIM)
        ft_ref[sl, :] = f[:, sl].T
        kt_ref[sl, :] = kk[:, sl].T
    q_ref[...] = qh.astype(BF16)
    v_ref[...] = proj(2)
    og_ref[...] = _silu(proj(3))

    lx = proj(4)
    cw = cw_ref[...]
    xc = cb_ref[...] + cbuf_ref[0] * cw[0:1]
    xc = xc + cbuf_ref[1] * cw[1:2]
    xc = xc + cbuf_ref[2] * cw[2:3]
    xc = xc + lx * cw[3:4]
    c_out_ref[0] = cbuf_ref[1]
    c_out_ref[1] = cbuf_ref[2]
    c_out_ref[2] = lx
    a, mult, gate_in = _lru_gates(xc, wa_ref, wx_ref, ba_ref[...], bx_ref[...], ll_ref[...])
    hnew = a * h0_ref[...] + mult * gate_in
    h_out_ref[...] = hnew
    hb_ref[...] = (_gelu_tanh(proj(5)) * hnew).astype(BF16)
    ga_ref[...] = jax.nn.sigmoid(proj(6))
    gb_ref[...] = jax.nn.sigmoid(proj(7))


def _sample_state_kernel(s_ref, ft_ref, kt_ref, q_ref, v_ref, s_out_ref, o_ref, *, bb):
    i = pl.program_id(0)
    lanes = ft_ref.shape[-1]
    shift = lax.rem(lanes - i * bb, lanes)
    ft = pltpu.roll(ft_ref[...], shift, 1)
    kt = pltpu.roll(kt_ref[...], shift, 1)
    for u in range(bb):
        for h in range(HEADS):
            sl = slice(h * HEAD_DIM, (h + 1) * HEAD_DIM)
            s_new = ft[sl, u:u + 1] * s_ref[u, h] + kt[sl, u:u + 1] * v_ref[u:u + 1, sl]
            s_out_ref[u, h] = s_new
            o_ref[u:u + 1, sl] = jnp.dot(q_ref[u:u + 1, sl], s_new.astype(BF16),
                                         preferred_element_type=F32)


def _sample_tail_kernel(
        x_ref, o_ref, og_ref, hb_ref, ga_ref, gb_ref, onorm_ref, g_post_ref, w_pa_ref, w_pb_ref,
        w_o_ref, g_fpre_ref, g_fpost_ref, w_fi_ref, w_fo_ref, y_ref):
    o = o_ref[...]
    og = og_ref[...]
    oa = jnp.concatenate(
        [_head_norm_gate(o[:, h * HEAD_DIM:(h + 1) * HEAD_DIM],
                         og[:, h * HEAD_DIM:(h + 1) * HEAD_DIM], onorm_ref[...])
         for h in range(HEADS)], axis=-1).astype(BF16)
    x1 = _merge_out(x_ref[...], oa, hb_ref[...], ga_ref[...], gb_ref[...], g_post_ref[...],
                    [w_pa_ref], [w_pb_ref], [w_o_ref])
    d_ff = w_fo_ref.shape[0]
    w_fi = _col_refs(w_fi_ref, d_ff)
    y_ref[...] = _ffn(x1, g_fpre_ref[...], g_fpost_ref[...], w_fi[:1], w_fi[1:], [w_fo_ref])


def _sample_step(x, s0, h0, cbuf, p, bb=16):
    nb, d = x.shape
    assert nb == HEAD_DIM and nb % bb == 0
    cp = pltpu.CompilerParams(vmem_limit_bytes=V7X_VMEM_LIMIT_BYTES)
    tok = jax.ShapeDtypeStruct((nb, d), F32)
    tok16 = jax.ShapeDtypeStruct((nb, d), BF16)
    col = jax.ShapeDtypeStruct((d, nb), F32)
    ft, kt, q, v, og, hb, ga, gb, h_new, c_new = pl.pallas_call(
        _sample_token_kernel,
        out_shape=(col, col, tok16, tok, tok, tok16, tok, tok, tok,
                   jax.ShapeDtypeStruct((CONV_W - 1, nb, d), F32)),
        name="sample_token", compiler_params=cp,
    )(x, h0, cbuf, p["g_mix_pre"], p["w_in"], p["lb_logits"], p["conv_w"], p["conv_b"],
      p["wa"], p["ba"], p["wx"], p["bx"], p["log_lambda"])

    s_new, o = pl.pallas_call(
        functools.partial(_sample_state_kernel, bb=bb), grid=(nb // bb,),
        in_specs=[pl.BlockSpec((bb, HEADS, HEAD_DIM, HEAD_DIM), lambda i: (i, 0, 0, 0)),
                  _full((d, nb)), _full((d, nb)),
                  pl.BlockSpec((bb, d), lambda i: (i, 0)), pl.BlockSpec((bb, d), lambda i: (i, 0))],
        out_specs=(pl.BlockSpec((bb, HEADS, HEAD_DIM, HEAD_DIM), lambda i: (i, 0, 0, 0)),
                   pl.BlockSpec((bb, d), lambda i: (i, 0))),
        out_shape=(jax.ShapeDtypeStruct(s0.shape, F32), tok),
        name="sample_state",
        compiler_params=pltpu.CompilerParams(
            dimension_semantics=("arbitrary",), vmem_limit_bytes=V7X_VMEM_LIMIT_BYTES),
    )(s0, ft, kt, q, v)

    y = pl.pallas_call(
        _sample_tail_kernel, out_shape=tok, name="sample_tail", compiler_params=cp,
    )(x, o, og, hb, ga, gb, p["onorm"], p["g_mix_post"], p["w_pa"], p["w_pb"], p["w_o"],
      p["g_ffn_pre"], p["g_ffn_post"], p["w_fi"], p["w_fo"])
    return y, s_new, h_new, c_new


def kernel(x_prompt, x_sample, state_hgrn, state_lru, state_conv, norm_mix_pre, norm_mix_post,
           norm_ffn_pre, norm_ffn_post, w_in, hg_lb_logits, hg_out_norm, lru_conv_w, lru_conv_b,
           lru_w_a, lru_b_a, lru_w_x, lru_b_x, lru_log_lambda, w_branch_a, w_branch_b, w_out,
           w_ffn_in, w_ffn_out):
    depth = w_in.shape[0]
    assert depth == 1 and hg_lb_logits.shape[0] == 2
    nb, t, d = x_prompt.shape
    nbs = x_sample.shape[0]
    assert x_sample.shape[1] == 1

    def row(a):
        return a.reshape(1, -1).astype(F32)

    p = dict(
        g_mix_pre=row(norm_mix_pre[0]), g_mix_post=row(norm_mix_post[0]),
        g_ffn_pre=row(norm_ffn_pre[0]), g_ffn_post=row(norm_ffn_post[0]),
        w_in=w_in[0].astype(BF16), lb_logits=hg_lb_logits.astype(F32),
        onorm=row(hg_out_norm[0]), conv_w=lru_conv_w[0].astype(F32), conv_b=row(lru_conv_b[0]),
        wa=lru_w_a[0].astype(BF16), ba=row(lru_b_a[0]), wx=lru_w_x[0].astype(BF16),
        bx=row(lru_b_x[0]), log_lambda=row(lru_log_lambda[0]),
        w_pa=w_branch_a[0].astype(BF16), w_pb=w_branch_b[0].astype(BF16),
        w_o=w_out[0].astype(BF16), w_fi=w_ffn_in[0].astype(BF16), w_fo=w_ffn_out[0].astype(BF16),
    )

    x1, s_p, h_p, c_p = _prompt_mixer(x_prompt, p, tm=256)
    y_p = _ffn_call(x1.reshape(nb * t, d), p, tm=512).reshape(nb, t, d)

    y_s, s_s, h_s, c_s = _sample_step(
        x_sample.reshape(nbs, d), state_hgrn[0], state_lru[0],
        jnp.swapaxes(state_conv[0], 0, 1), p)

    return (y_p, y_s.reshape(nbs, 1, d), s_p[None], h_p.reshape(1, nb, d), c_p[None],
            s_s[None], h_s[None], jnp.swapaxes(c_s, 0, 1)[None])
```

```python
import functools

import jax
import jax.numpy as jnp
import numpy as np
from jax import lax
from jax.experimental import pallas as pl
from jax.experimental.pallas import tpu as pltpu

F32 = jnp.float32
BF16 = jnp.bfloat16

HEADS = 8
HEAD_DIM = 128
LRU_BLOCKS = 8
LRU_BLOCK_W = 128
CONV_W = 4
LRU_C = 8.0
EPS = 1e-6
CHUNK = 64
SUB = 16
SLAB = 8
PROJ_COLS = 512
FFN_COLS = 256
V7X_VMEM_LIMIT_BYTES = 56 * 1024 * 1024


def _rms(x, g):
    ms = jnp.mean(x * x, axis=-1, keepdims=True)
    return x * lax.rsqrt(ms + EPS) * g


def _bdot(a, w):
    return jnp.dot(a.astype(BF16), w, preferred_element_type=F32)


def _dot_nt(a, b):
    return lax.dot_general(a, b, (((1,), (1,)), ((), ())), preferred_element_type=F32)


def _dot_tn(a, b):
    return lax.dot_general(a, b, (((0,), (0,)), ((), ())), preferred_element_type=F32)


def _silu(x):
    return x * jax.nn.sigmoid(x)


def _gelu_tanh(x):
    c = 0.7978845608028654
    return x * (0.5 * (1.0 + jnp.tanh(c * (x + 0.044715 * (x * x * x)))))


def _softplus(z):
    return jnp.maximum(z, 0.0) + jnp.log1p(jnp.exp(-jnp.abs(z)))


def _forget_lower_bound(logits):
    m = jnp.max(logits, axis=0, keepdims=True)
    e = jnp.exp(logits - m)
    return e[0:1] / (e[0:1] + e[1:2])


def _lru_gates(xc, wa_ref, wx_ref, ba, bx, log_lambda):
    ra, ix = [], []
    for n in range(LRU_BLOCKS):
        sl = slice(n * LRU_BLOCK_W, (n + 1) * LRU_BLOCK_W)
        xb = xc[:, sl].astype(BF16)
        ra.append(jnp.dot(xb, wa_ref[n], preferred_element_type=F32))
        ix.append(jnp.dot(xb, wx_ref[n], preferred_element_type=F32))
    ra = jnp.concatenate(ra, axis=-1) + ba
    ix = jnp.concatenate(ix, axis=-1) + bx
    log_a = (-LRU_C) * jax.nn.sigmoid(ra) * _softplus(-log_lambda)
    a = jnp.exp(log_a)
    mult = jnp.sqrt(-jnp.tanh(log_a) * (a * a + 1.0))
    gate_in = jax.nn.sigmoid(ix) * xc
    return a, mult, gate_in


def _head_norm_gate(oh, og_silu, onorm):
    ms = jnp.mean(oh * oh, axis=-1, keepdims=True)
    return oh * lax.rsqrt(ms + EPS) * onorm * og_silu


def _col_refs(w_ref, cols):
    return [w_ref.at[:, c * cols:(c + 1) * cols] for c in range(w_ref.shape[1] // cols)]


def _col_specs(w, cols):
    n = w.shape[1] // cols
    assert n * cols == w.shape[1]
    specs = [pl.BlockSpec((w.shape[0], cols), functools.partial(lambda c, *_: (0, c), c),
                          pipeline_mode=pl.Buffered(1)) for c in range(n)]
    return specs, [w] * n


def _cdot(a, w_chunks):
    parts = [jnp.dot(a, w[...], preferred_element_type=F32) for w in w_chunks]
    return parts[0] if len(parts) == 1 else jnp.concatenate(parts, axis=-1)


def _merge_out(x, oa, hb, ga_sig, gb_sig, g_post, w_pa, w_pb, w_o):
    merged = ga_sig * _cdot(oa, w_pa) + gb_sig * _cdot(hb, w_pb)
    return x + _rms(_cdot(merged.astype(BF16), w_o), g_post)


def _ffn(x, g_pre, g_post, w_gate, w_up, w_fo):
    hn = _rms(x, g_pre).astype(BF16)
    act = jnp.concatenate(
        [(_silu(_cdot(hn, [wg])) * _cdot(hn, [wu])).astype(BF16)
         for wg, wu in zip(w_gate, w_up, strict=True)], axis=-1)
    return x + _rms(_cdot(act, w_fo), g_post)


def _chunk_prefix_sum(b2_ref, r0, sl, row8):
    slabs, carry = [], None
    for i in range(CHUNK // SLAB):
        vv = b2_ref[r0 + i * SLAB:r0 + (i + 1) * SLAB, sl]
        for s in (1, 2, 4):
            vv = vv + jnp.where(row8 >= s, pltpu.roll(vv, s, 0), 0.0)
        if carry is not None:
            vv = vv + carry
        carry = vv[SLAB - 1:SLAB]
        slabs.append(vv)
    return jnp.concatenate(slabs, axis=0)


def _prompt_mixer_kernel(*refs, tm):
    (x_ref, g_pre_ref, g_post_ref, lbl_ref, onorm_ref, cw_ref, cb_ref,
     wa_ref, ba_ref, wx_ref, bx_ref, ll_ref, pin_ref, pout_ref) = refs[:14]
    d = x_ref.shape[-1]
    per = d // PROJ_COLS
    weights = refs[14:14 + 11 * per]
    w_in, w_pa, w_pb, w_o = (weights[:8 * per], weights[8 * per:9 * per],
                             weights[9 * per:10 * per], weights[10 * per:])
    (y_ref, s_out_ref, h_out_ref, c_out_ref,
     st_ref, h_ref, cbuf_s, xn_s, qh_s, b2_s, kk_s, v_s, og_s, ly_s, ga_s, gb_s, hg_s, hb_s,
     qin_s, a_s, u_s, dec_s, la_s, lb_s, xnp_s, lxp_s) = refs[14 + 11 * per:]
    j = pl.program_id(1)
    ngrp = tm // SLAB
    nchunk = tm // CHUNK
    nsub = CHUNK // SUB

    @pl.when(j == 0)
    def _():
        st_ref[...] = jnp.zeros_like(st_ref)
        h_ref[...] = jnp.zeros_like(h_ref)
        cbuf_s[...] = jnp.zeros_like(cbuf_s)

    always = j >= 0

    def proj(g, c0, interleaved=False):
        lhs = xnp_s[...] if interleaved else xn_s[...]
        return jnp.dot(lhs, w_in[(g * d + c0) // PROJ_COLS][...], preferred_element_type=F32)

    def project_recurrence_inputs():
        xn_s[...] = _rms(x_ref[0], g_pre_ref[...]).astype(BF16)
        xnp_s[...] = jnp.dot(pin_ref[...], xn_s[...], preferred_element_type=F32).astype(BF16)
        for c0 in range(0, d, PROJ_COLS):
            lxp_s[:, c0:c0 + PROJ_COLS] = proj(4, c0, interleaved=True)
        lb = _forget_lower_bound(lbl_ref[...])
        for c0 in range(0, d, PROJ_COLS):
            cs = slice(c0, c0 + PROJ_COLS)
            qh_s[:, cs] = _silu(proj(0, c0))
            f = lb[:, cs] + (1.0 - lb[:, cs]) * jax.nn.sigmoid(proj(1, c0))
            b2_s[:, cs] = jnp.log2(f)
            kk_s[:, cs] = 1.0 - f
            v_s[:, cs] = proj(2, c0).astype(BF16)

    def rglru_gates():
        cw = cw_ref[...]
        cb = cb_ref[...]
        row8d = lax.broadcasted_iota(jnp.int32, (SLAB, d), 0)

        def grp(g):
            return lxp_s[g * SLAB:(g + 1) * SLAB, :]

        def one_token_back(v, first):
            return jnp.where(row8d == 0, first, pltpu.roll(v, 1, 0))

        carry = cbuf_s[...]
        wrap = [one_token_back(grp(ngrp - k), carry[SLAB - k:SLAB - k + 1]) for k in (1, 2, 3)]

        def back(g, k):
            return grp(g - k) if g >= k else wrap[k - g - 1]

        xc = []
        for g in range(ngrp):
            acc = cb + back(g, 3) * cw[0:1]
            acc = acc + back(g, 2) * cw[1:2]
            acc = acc + back(g, 1) * cw[2:3]
            xc.append(acc + grp(g) * cw[3:4])
        for k in (1, 2, 3):
            cbuf_s[SLAB - k:SLAB - k + 1, :] = grp(ngrp - k)[SLAB - 1:SLAB]
        xc = jnp.concatenate(xc, axis=0)
        a, mult, gate_in = _lru_gates(xc, wa_ref, wx_ref, ba_ref[...], bx_ref[...], ll_ref[...])
        row = lax.broadcasted_iota(jnp.int32, (tm, d), 0)
        mult = jnp.where(jnp.logical_and(row == 0, j == 0), 1.0, mult)
        la_s[...] = a
        lb_s[...] = mult * gate_in

    def project_gates():
        for c0 in range(0, d, PROJ_COLS):
            cs = slice(c0, c0 + PROJ_COLS)
            og_s[:, cs] = _silu(proj(3, c0))
            ly_s[:, cs] = _gelu_tanh(proj(5, c0, interleaved=True))
            ga_s[:, cs] = jax.nn.sigmoid(proj(6, c0))
            gb_s[:, cs] = jax.nn.sigmoid(proj(7, c0))

    def hgrn2_scores():
        row8 = lax.broadcasted_iota(jnp.int32, (SLAB, HEAD_DIM), 0)
        for h in range(HEADS):
            sl = slice(h * HEAD_DIM, (h + 1) * HEAD_DIM)
            for c in range(nchunk):
                r0 = c * CHUNK
                rs = slice(r0, r0 + CHUNK)
                bc = _chunk_prefix_sum(b2_s, r0, sl, row8)
                qc, kc, vc = qh_s[rs, sl], kk_s[rs, sl], v_s[rs, sl]
                b_last = bc[CHUNK - 1:CHUNK]
                qin_s[rs, sl] = (qc * jnp.exp2(bc)).astype(BF16)
                dec_s[c:c + 1, sl] = jnp.exp2(b_last)
                for ib in range(nsub):
                    t0, t1 = ib * SUB, (ib + 1) * SUB
                    ref_b = bc[t0 + SUB // 2 - 1:t0 + SUB // 2]
                    qs = (qc[t0:t1] * jnp.exp2(bc[t0:t1] - ref_b)).astype(BF16)
                    ks = (kc[:t1] * jnp.exp2(ref_b - bc[:t1])).astype(BF16)
                    a_blk = _dot_nt(qs, ks)
                    ti = lax.broadcasted_iota(jnp.int32, a_blk.shape, 0) + t0
                    si = lax.broadcasted_iota(jnp.int32, a_blk.shape, 1)
                    a_s[r0 + t0:r0 + t1, h * HEAD_DIM:h * HEAD_DIM + t1] = (
                        jnp.where(ti >= si, a_blk, 0.0).astype(BF16))
                kst = (kc * jnp.exp2(b_last - bc)).astype(BF16)
                u_s[h * nchunk + c] = _dot_tn(vc, kst)

    def hgrn2_recurrence():
        for h in range(HEADS):
            sl = slice(h * HEAD_DIM, (h + 1) * HEAD_DIM)
            st = st_ref[h]
            for c in range(nchunk):
                r0 = c * CHUNK
                rs = slice(r0, r0 + CHUNK)
                o_c = _dot_nt(qin_s[rs, sl], st.astype(BF16))
                rows = []
                for ib in range(nsub):
                    t0, t1 = ib * SUB, (ib + 1) * SUB
                    a_blk = a_s[r0 + t0:r0 + t1, h * HEAD_DIM:h * HEAD_DIM + t1]
                    rows.append(o_c[t0:t1] + jnp.dot(a_blk, v_s[r0:r0 + t1, sl],
                                                     preferred_element_type=F32))
                hg_s[rs, sl] = jnp.concatenate(rows, axis=0)
                st = dec_s[c:c + 1, sl] * st + u_s[h * nchunk + c]
            st_ref[h] = st

    def rglru_scan():
        def rows(g):
            return slice(g * SLAB, (g + 1) * SLAB)

        pa, pb = la_s[rows(0), :], lb_s[rows(0), :]
        for g in range(1, ngrp):
            ag = la_s[rows(g), :]
            pb = ag * pb + lb_s[rows(g), :]
            pa = ag * pa
            la_s[rows(g), :] = pa
            lb_s[rows(g), :] = pb
        enter = [h_ref[...]]
        for r in range(1, SLAB):
            enter.append(pa[r - 1:r] * enter[-1] + pb[r - 1:r])
        h_ref[...] = pa[SLAB - 1:SLAB] * enter[-1] + pb[SLAB - 1:SLAB]
        enter = jnp.concatenate(enter, axis=0)
        hb = [(ly_s[rows(g), :] * (la_s[rows(g), :] * enter + lb_s[rows(g), :])).astype(BF16)
              for g in range(ngrp)]
        hb = jnp.dot(pout_ref[...], jnp.concatenate(hb, axis=0), preferred_element_type=F32)
        hb_s[...] = hb.astype(BF16)

    def merge_out():
        onorm = onorm_ref[...]
        oa = jnp.concatenate(
            [_head_norm_gate(hg_s[:, h * HEAD_DIM:(h + 1) * HEAD_DIM],
                             og_s[:, h * HEAD_DIM:(h + 1) * HEAD_DIM], onorm)
             for h in range(HEADS)], axis=-1).astype(BF16)
        y_ref[0] = _merge_out(x_ref[0], oa, hb_s[...], ga_s[...], gb_s[...], g_post_ref[...],
                              w_pa, w_pb, w_o)

    @pl.when(always)
    def _():
        project_recurrence_inputs()
        rglru_gates()

    @pl.when(always)
    def _():
        hgrn2_scores()
        project_gates()

    @pl.when(always)
    def _():
        hgrn2_recurrence()
        rglru_scan()

    @pl.when(always)
    def _():
        merge_out()

    @pl.when(j == pl.num_programs(1) - 1)
    def _():
        for h in range(HEADS):
            s_out_ref[0, h] = st_ref[h].T
        h_out_ref[0] = h_ref[...]
        c_out_ref[0] = cbuf_s[SLAB - CONV_W + 1:SLAB, :]


def _full(shape):
    return pl.BlockSpec(shape, lambda *_: (0,) * len(shape))


def _prompt_mixer(x, p, tm):
    nb, t, d = x.shape
    assert t % tm == 0 and tm % CHUNK == 0 and d == HEADS * HEAD_DIM and d % PROJ_COLS == 0
    assert tm // CHUNK <= SLAB
    kern = functools.partial(_prompt_mixer_kernel, tm=tm)
    in_specs = [
        pl.BlockSpec((1, tm, d), lambda s, j: (s, j, 0)),
        _full((1, d)), _full((1, d)), _full(p["lb_logits"].shape),
        _full((1, HEAD_DIM)), _full((CONV_W, d)), _full((1, d)),
        _full(p["wa"].shape), _full((1, d)), _full(p["wx"].shape), _full((1, d)), _full((1, d)),
        _full((tm, tm)), _full((tm, tm)),
    ]
    ngrp = tm // SLAB
    rows = np.arange(tm)
    pin = np.zeros((tm, tm), np.float32)
    pin[rows, (rows % SLAB) * ngrp + rows // SLAB] = 1.0
    pin, pout = jnp.asarray(pin, BF16), jnp.asarray(pin.T, BF16)
    weights = []
    for name in ("w_in", "w_pa", "w_pb", "w_o"):
        specs, ops = _col_specs(p[name], PROJ_COLS)
        in_specs += specs
        weights += ops
    out_shape = (
        jax.ShapeDtypeStruct((nb, t, d), F32),
        jax.ShapeDtypeStruct((nb, HEADS, HEAD_DIM, HEAD_DIM), F32),
        jax.ShapeDtypeStruct((nb, 1, d), F32),
        jax.ShapeDtypeStruct((nb, CONV_W - 1, d), F32),
    )
    out_specs = (
        pl.BlockSpec((1, tm, d), lambda s, j: (s, j, 0)),
        pl.BlockSpec((1, HEADS, HEAD_DIM, HEAD_DIM), lambda s, j: (s, 0, 0, 0)),
        pl.BlockSpec((1, 1, d), lambda s, j: (s, 0, 0)),
        pl.BlockSpec((1, CONV_W - 1, d), lambda s, j: (s, 0, 0)),
    )
    tile_f32 = pltpu.VMEM((tm, d), F32)
    tile_bf16 = pltpu.VMEM((tm, d), BF16)
    scratch = [
        pltpu.VMEM((HEADS, HEAD_DIM, HEAD_DIM), F32),
        pltpu.VMEM((1, d), F32),
        pltpu.VMEM((SLAB, d), F32),
        tile_bf16,
        tile_f32, tile_f32, tile_f32,
        tile_bf16,
        tile_f32, tile_f32, tile_f32, tile_f32,
        tile_f32,
        tile_bf16,
        tile_bf16,
        tile_bf16,
        pltpu.VMEM((HEADS * (tm // CHUNK), HEAD_DIM, HEAD_DIM), F32),
        pltpu.VMEM((SLAB, d), F32),
        tile_f32, tile_f32,
        tile_bf16, tile_f32,
    ]
    return pl.pallas_call(
        kern, grid=(nb, t // tm), in_specs=in_specs, out_specs=out_specs, out_shape=out_shape,
        scratch_shapes=scratch, name="prompt_mixer",
        compiler_params=pltpu.CompilerParams(
            dimension_semantics=("arbitrary", "arbitrary"),
            vmem_limit_bytes=V7X_VMEM_LIMIT_BYTES),
    )(x, p["g_mix_pre"], p["g_mix_post"], p["lb_logits"], p["onorm"], p["conv_w"],
      p["conv_b"], p["wa"], p["ba"], p["wx"], p["bx"], p["log_lambda"], pin, pout, *weights)


def _ffn_kernel(*refs, nf):
    x_ref, g_pre_ref, g_post_ref = refs[:3]
    w_gate, w_up, w_fo = refs[3:3 + nf], refs[3 + nf:3 + 2 * nf], refs[3 + 2 * nf:-1]
    refs[-1][...] = _ffn(x_ref[...], g_pre_ref[...], g_post_ref[...], w_gate, w_up, w_fo)


def _ffn_call(x, p, tm):
    n, d = x.shape
    assert n % tm == 0
    fi_specs, fi_ops = _col_specs(p["w_fi"], FFN_COLS)
    fo_specs, fo_ops = _col_specs(p["w_fo"], PROJ_COLS)
    return pl.pallas_call(
        functools.partial(_ffn_kernel, nf=len(fi_ops) // 2), grid=(n // tm,),
        in_specs=[pl.BlockSpec((tm, d), lambda i: (i, 0)), _full((1, d)), _full((1, d))]
        + fi_specs + fo_specs,
        out_specs=pl.BlockSpec((tm, d), lambda i: (i, 0)),
        out_shape=jax.ShapeDtypeStruct((n, d), F32), name="swiglu",
        compiler_params=pltpu.CompilerParams(
            dimension_semantics=("arbitrary",), vmem_limit_bytes=V7X_VMEM_LIMIT_BYTES),
    )(x, p["g_ffn_pre"], p["g_ffn_post"], *fi_ops, *fo_ops)


def _sample_token_kernel(
        x_ref, h0_ref, cbuf_ref, g_pre_ref, w_in_ref, lbl_ref, cw_ref, cb_ref,
        wa_ref, ba_ref, wx_ref, bx_ref, ll_ref,
        ft_ref, kt_ref, q_ref, v_ref, og_ref, hb_ref, ga_ref, gb_ref, h_out_ref, c_out_ref):
    d = x_ref.shape[-1]
    xn = _rms(x_ref[...], g_pre_ref[...]).astype(BF16)

    def proj(i):
        return _cdot(xn, [w_in_ref.at[:, i * d:(i + 1) * d]])

    lb = _forget_lower_bound(lbl_ref[...])
    qh = _silu(proj(0))
    f = lb + (1.0 - lb) * jax.nn.sigmoid(proj(1))
    kk = 1.0 - f
    for h in range(HEADS):
        sl = slice(h * HEAD_DIM, (h + 1) * HEAD_DIM)
        ft_ref[sl, :] = f[:, sl].T
        kt_ref[sl, :] = kk[:, sl].T
    q_ref[...] = qh.astype(BF16)
    v_ref[...] = proj(2)
    og_ref[...] = _silu(proj(3))

    lx = proj(4)
    cw = cw_ref[...]
    xc = cb_ref[...] + cbuf_ref[0] * cw[0:1]
    xc = xc + cbuf_ref[1] * cw[1:2]
    xc = xc + cbuf_ref[2] * cw[2:3]
    xc = xc + lx * cw[3:4]
    c_out_ref[0] = cbuf_ref[1]
    c_out_ref[1] = cbuf_ref[2]
    c_out_ref[2] = lx
    a, mult, gate_in = _lru_gates(xc, wa_ref, wx_ref, ba_ref[...], bx_ref[...], ll_ref[...])
    hnew = a * h0_ref[...] + mult * gate_in
    h_out_ref[...] = hnew
    hb_ref[...] = (_gelu_tanh(proj(5)) * hnew).astype(BF16)
    ga_ref[...] = jax.nn.sigmoid(proj(6))
    gb_ref[...] = jax.nn.sigmoid(proj(7))


def _sample_state_kernel(s_ref, ft_ref, kt_ref, q_ref, v_ref, s_out_ref, o_ref, *, bb):
    i = pl.program_id(0)
    lanes = ft_ref.shape[-1]
    shift = lax.rem(lanes - i * bb, lanes)
    ft = pltpu.roll(ft_ref[...], shift, 1)
    kt = pltpu.roll(kt_ref[...], shift, 1)
    for u in range(bb):
        for h in range(HEADS):
            sl = slice(h * HEAD_DIM, (h + 1) * HEAD_DIM)
            s_new = ft[sl, u:u + 1] * s_ref[u, h] + kt[sl, u:u + 1] * v_ref[u:u + 1, sl]
            s_out_ref[u, h] = s_new
            o_ref[u:u + 1, sl] = jnp.dot(q_ref[u:u + 1, sl], s_new.astype(BF16),
                                         preferred_element_type=F32)


def _sample_tail_kernel(
        x_ref, o_ref, og_ref, hb_ref, ga_ref, gb_ref, onorm_ref, g_post_ref, w_pa_ref, w_pb_ref,
        w_o_ref, g_fpre_ref, g_fpost_ref, w_fi_ref, w_fo_ref, y_ref):
    o = o_ref[...]
    og = og_ref[...]
    oa = jnp.concatenate(
        [_head_norm_gate(o[:, h * HEAD_DIM:(h + 1) * HEAD_DIM],
                         og[:, h * HEAD_DIM:(h + 1) * HEAD_DIM], onorm_ref[...])
         for h in range(HEADS)], axis=-1).astype(BF16)
    x1 = _merge_out(x_ref[...], oa, hb_ref[...], ga_ref[...], gb_ref[...], g_post_ref[...],
                    [w_pa_ref], [w_pb_ref], [w_o_ref])
    d_ff = w_fo_ref.shape[0]
    w_fi = _col_refs(w_fi_ref, d_ff)
    y_ref[...] = _ffn(x1, g_fpre_ref[...], g_fpost_ref[...], w_fi[:1], w_fi[1:], [w_fo_ref])


def _sample_step(x, s0, h0, cbuf, p, bb=16):
    nb, d = x.shape
    assert nb == HEAD_DIM and nb % bb == 0
    cp = pltpu.CompilerParams(vmem_limit_bytes=V7X_VMEM_LIMIT_BYTES)
    tok = jax.ShapeDtypeStruct((nb, d), F32)
    tok16 = jax.ShapeDtypeStruct((nb, d), BF16)
    col = jax.ShapeDtypeStruct((d, nb), F32)
    ft, kt, q, v, og, hb, ga, gb, h_new, c_new = pl.pallas_call(
        _sample_token_kernel,
        out_shape=(col, col, tok16, tok, tok, tok16, tok, tok, tok,
                   jax.ShapeDtypeStruct((CONV_W - 1, nb, d), F32)),
        name="sample_token", compiler_params=cp,
    )(x, h0, cbuf, p["g_mix_pre"], p["w_in"], p["lb_logits"], p["conv_w"], p["conv_b"],
      p["wa"], p["ba"], p["wx"], p["bx"], p["log_lambda"])

    s_new, o = pl.pallas_call(
        functools.partial(_sample_state_kernel, bb=bb), grid=(nb // bb,),
        in_specs=[pl.BlockSpec((bb, HEADS, HEAD_DIM, HEAD_DIM), lambda i: (i, 0, 0, 0)),
                  _full((d, nb)), _full((d, nb)),
                  pl.BlockSpec((bb, d), lambda i: (i, 0)), pl.BlockSpec((bb, d), lambda i: (i, 0))],
        out_specs=(pl.BlockSpec((bb, HEADS, HEAD_DIM, HEAD_DIM), lambda i: (i, 0, 0, 0)),
                   pl.BlockSpec((bb, d), lambda i: (i, 0))),
        out_shape=(jax.ShapeDtypeStruct(s0.shape, F32), tok),
        name="sample_state",
        compiler_params=pltpu.CompilerParams(
            dimension_semantics=("arbitrary",), vmem_limit_bytes=V7X_VMEM_LIMIT_BYTES),
    )(s0, ft, kt, q, v)

    y = pl.pallas_call(
        _sample_tail_kernel, out_shape=tok, name="sample_tail", compiler_params=cp,
    )(x, o, og, hb, ga, gb, p["onorm"], p["g_mix_post"], p["w_pa"], p["w_pb"], p["w_o"],
      p["g_ffn_pre"], p["g_ffn_post"], p["w_fi"], p["w_fo"])
    return y, s_new, h_new, c_new


def kernel(x_prompt, x_sample, state_hgrn, state_lru, state_conv, norm_mix_pre, norm_mix_post,
           norm_ffn_pre, norm_ffn_post, w_in, hg_lb_logits, hg_out_norm, lru_conv_w, lru_conv_b,
           lru_w_a, lru_b_a, lru_w_x, lru_b_x, lru_log_lambda, w_branch_a, w_branch_b, w_out,
           w_ffn_in, w_ffn_out):
    depth = w_in.shape[0]
    assert depth == 1 and hg_lb_logits.shape[0] == 2
    nb, t, d = x_prompt.shape
    nbs = x_sample.shape[0]
    assert x_sample.shape[1] == 1

    def row(a):
        return a.reshape(1, -1).astype(F32)

    p = dict(
        g_mix_pre=row(norm_mix_pre[0]), g_mix_post=row(norm_mix_post[0]),
        g_ffn_pre=row(norm_ffn_pre[0]), g_ffn_post=row(norm_ffn_post[0]),
        w_in=w_in[0].astype(BF16), lb_logits=hg_lb_logits.astype(F32),
        onorm=row(hg_out_norm[0]), conv_w=lru_conv_w[0].astype(F32), conv_b=row(lru_conv_b[0]),
        wa=lru_w_a[0].astype(BF16), ba=row(lru_b_a[0]), wx=lru_w_x[0].astype(BF16),
        bx=row(lru_b_x[0]), log_lambda=row(lru_log_lambda[0]),
        w_pa=w_branch_a[0].astype(BF16), w_pb=w_branch_b[0].astype(BF16),
        w_o=w_out[0].astype(BF16), w_fi=w_ffn_in[0].astype(BF16), w_fo=w_ffn_out[0].astype(BF16),
    )

    x1, s_p, h_p, c_p = _prompt_mixer(x_prompt, p, tm=256)
    y_p = _ffn_call(x1.reshape(nb * t, d), p, tm=512).reshape(nb, t, d)

    y_s, s_s, h_s, c_s = _sample_step(
        x_sample.reshape(nbs, d), state_hgrn[0], state_lru[0],
        jnp.swapaxes(state_conv[0], 0, 1), p)

    return (y_p, y_s.reshape(nbs, 1, d), s_p[None], h_p.reshape(1, nb, d), c_p[None],
            s_s[None], h_s[None], jnp.swapaxes(c_s, 0, 1)[None])
```

```python
import functools

import jax
import jax.numpy as jnp
from jax import lax
from jax.experimental import pallas as pl
from jax.experimental.pallas import tpu as pltpu

F32 = jnp.float32
BF16 = jnp.bfloat16

HEADS = 8
HEAD_DIM = 128
LRU_BLOCKS = 8
LRU_BLOCK_W = 128
CONV_W = 4
LRU_C = 8.0
EPS = 1e-6
CHUNK = 64
SUB = 16
SLAB = 8
PROJ_COLS = 512
FFN_COLS = 256
V7X_VMEM_LIMIT_BYTES = 56 * 1024 * 1024


def _rms(x, g):
    ms = jnp.mean(x * x, axis=-1, keepdims=True)
    return x * lax.rsqrt(ms + EPS) * g


def _dot_nt(a, b):
    return lax.dot_general(a, b, (((1,), (1,)), ((), ())), preferred_element_type=F32)


def _dot_tn(a, b):
    return lax.dot_general(a, b, (((0,), (0,)), ((), ())), preferred_element_type=F32)


def _silu(x):
    return x * jax.nn.sigmoid(x)


def _gelu_tanh(x):
    c = 0.7978845608028654
    return x * (0.5 * (1.0 + jnp.tanh(c * (x + 0.044715 * (x * x * x)))))


def _softplus(z):
    return jnp.maximum(z, 0.0) + jnp.log1p(jnp.exp(-jnp.abs(z)))


def _forget_lower_bound(logits):
    m = jnp.max(logits, axis=0, keepdims=True)
    e = jnp.exp(logits - m)
    return e[0:1] / (e[0:1] + e[1:2])


def _lru_gates(xc, wa_ref, wx_ref, ba, bx, log_lambda):
    ra, ix = [], []
    for n in range(LRU_BLOCKS):
        sl = slice(n * LRU_BLOCK_W, (n + 1) * LRU_BLOCK_W)
        xb = xc[:, sl].astype(BF16)
        ra.append(jnp.dot(xb, wa_ref[n], preferred_element_type=F32))
        ix.append(jnp.dot(xb, wx_ref[n], preferred_element_type=F32))
    ra = jnp.concatenate(ra, axis=-1) + ba
    ix = jnp.concatenate(ix, axis=-1) + bx
    log_a = (-LRU_C) * jax.nn.sigmoid(ra) * _softplus(-log_lambda)
    a = jnp.exp(log_a)
    mult = jnp.sqrt(-jnp.tanh(log_a) * (a * a + 1.0))
    gate_in = jax.nn.sigmoid(ix) * xc
    return a, mult, gate_in


def _head_norm_gate(oh, og_silu, onorm):
    ms = jnp.mean(oh * oh, axis=-1, keepdims=True)
    return oh * lax.rsqrt(ms + EPS) * onorm * og_silu


def _col_refs(w_ref, cols):
    return [w_ref.at[:, c * cols:(c + 1) * cols] for c in range(w_ref.shape[1] // cols)]


def _col_specs(w, cols):
    n = w.shape[1] // cols
    assert n * cols == w.shape[1]
    specs = [pl.BlockSpec((w.shape[0], cols), functools.partial(lambda c, *_: (0, c), c),
                          pipeline_mode=pl.Buffered(1)) for c in range(n)]
    return specs, [w] * n


def _cdot(a, w_chunks):
    parts = [jnp.dot(a, w[...], preferred_element_type=F32) for w in w_chunks]
    return parts[0] if len(parts) == 1 else jnp.concatenate(parts, axis=-1)


def _merge_out(x, oa, hb, ga_sig, gb_sig, g_post, w_pa, w_pb, w_o):
    merged = ga_sig * _cdot(oa, w_pa) + gb_sig * _cdot(hb, w_pb)
    return x + _rms(_cdot(merged.astype(BF16), w_o), g_post)


def _ffn(x, g_pre, g_post, w_gate, w_up, w_fo):
    hn = _rms(x, g_pre).astype(BF16)
    act = jnp.concatenate(
        [(_silu(_cdot(hn, [wg])) * _cdot(hn, [wu])).astype(BF16)
         for wg, wu in zip(w_gate, w_up, strict=True)], axis=-1)
    return x + _rms(_cdot(act, w_fo), g_post)


def _chunk_prefix_sum(b2_ref, r0, sl, row8):
    slabs, carry = [], None
    for i in range(CHUNK // SLAB):
        vv = b2_ref[r0 + i * SLAB:r0 + (i + 1) * SLAB, sl]
        for s in (1, 2, 4):
            vv = vv + jnp.where(row8 >= s, pltpu.roll(vv, s, 0), 0.0)
        if carry is not None:
            vv = vv + carry
        carry = vv[SLAB - 1:SLAB]
        slabs.append(vv)
    return jnp.concatenate(slabs, axis=0)


def _prompt_mixer_kernel(*refs, tm):
    (x_ref, g_pre_ref, g_post_ref, lbl_ref, onorm_ref, cw_ref, cb_ref,
     wa_ref, ba_ref, wx_ref, bx_ref, ll_ref) = refs[:12]
    d = x_ref.shape[-1]
    per = d // PROJ_COLS
    weights = refs[12:12 + 11 * per]
    w_in, w_pa, w_pb, w_o = (weights[:8 * per], weights[8 * per:9 * per],
                             weights[9 * per:10 * per], weights[10 * per:])
    (y_ref, s_out_ref, h_out_ref, c_out_ref,
     st_ref, h_ref, xp_ref, xn_s, qh_s, b2_s, kk_s, v_s, og_s, ly_s, ga_s, gb_s, hg_s, hs_s,
     qin_s, a_s, u_s, dec_s, la_s, lb_s) = refs[12 + 11 * per:]
    j = pl.program_id(1)
    nchunk = tm // CHUNK
    nsub = CHUNK // SUB

    @pl.when(j == 0)
    def _():
        st_ref[...] = jnp.zeros_like(st_ref)
        h_ref[...] = jnp.zeros_like(h_ref)
        xp_ref[0:SLAB, :] = jnp.zeros((SLAB, d), F32)

    def proj(g, c0):
        return jnp.dot(xn_s[...], w_in[(g * d + c0) // PROJ_COLS][...],
                       preferred_element_type=F32)

    def project_recurrence_inputs():
        xn_s[...] = _rms(x_ref[0], g_pre_ref[...]).astype(BF16)
        for c0 in range(0, d, PROJ_COLS):
            xp_ref[SLAB:SLAB + tm, c0:c0 + PROJ_COLS] = proj(4, c0)
        lb = _forget_lower_bound(lbl_ref[...])
        for c0 in range(0, d, PROJ_COLS):
            cs = slice(c0, c0 + PROJ_COLS)
            qh_s[:, cs] = _silu(proj(0, c0))
            f = lb[:, cs] + (1.0 - lb[:, cs]) * jax.nn.sigmoid(proj(1, c0))
            b2_s[:, cs] = jnp.log2(f)
            kk_s[:, cs] = 1.0 - f
            v_s[:, cs] = proj(2, c0).astype(BF16)

    def rglru_gates():
        cw = cw_ref[...]
        xc = cb_ref[...] + xp_ref[5:5 + tm, :] * cw[0:1]
        xc = xc + xp_ref[6:6 + tm, :] * cw[1:2]
        xc = xc + xp_ref[7:7 + tm, :] * cw[2:3]
        xc = xc + xp_ref[8:8 + tm, :] * cw[3:4]
        xp_ref[0:SLAB, :] = xp_ref[tm:tm + SLAB, :]
        a, mult, gate_in = _lru_gates(xc, wa_ref, wx_ref, ba_ref[...], bx_ref[...], ll_ref[...])
        row = lax.broadcasted_iota(jnp.int32, (tm, d), 0)
        mult = jnp.where(jnp.logical_and(row == 0, j == 0), 1.0, mult)
        la_s[...] = a
        lb_s[...] = mult * gate_in

    def project_gates():
        for c0 in range(0, d, PROJ_COLS):
            cs = slice(c0, c0 + PROJ_COLS)
            og_s[:, cs] = _silu(proj(3, c0))
            ly_s[:, cs] = _gelu_tanh(proj(5, c0))
            ga_s[:, cs] = jax.nn.sigmoid(proj(6, c0))
            gb_s[:, cs] = jax.nn.sigmoid(proj(7, c0))

    def hgrn2_scores():
        row8 = lax.broadcasted_iota(jnp.int32, (SLAB, HEAD_DIM), 0)
        for h in range(HEADS):
            sl = slice(h * HEAD_DIM, (h + 1) * HEAD_DIM)
            for c in range(nchunk):
                r0 = c * CHUNK
                rs = slice(r0, r0 + CHUNK)
                bc = _chunk_prefix_sum(b2_s, r0, sl, row8)
                qc, kc, vc = qh_s[rs, sl], kk_s[rs, sl], v_s[rs, sl]
                b_last = bc[CHUNK - 1:CHUNK]
                qin_s[rs, sl] = (qc * jnp.exp2(bc)).astype(BF16)
                dec_s[c:c + 1, sl] = jnp.exp2(b_last)
                for ib in range(nsub):
                    t0, t1 = ib * SUB, (ib + 1) * SUB
                    ref_b = bc[t0 + SUB // 2 - 1:t0 + SUB // 2]
                    qs = (qc[t0:t1] * jnp.exp2(bc[t0:t1] - ref_b)).astype(BF16)
                    ks = (kc[:t1] * jnp.exp2(ref_b - bc[:t1])).astype(BF16)
                    a_blk = _dot_nt(qs, ks)
                    ti = lax.broadcasted_iota(jnp.int32, a_blk.shape, 0) + t0
                    si = lax.broadcasted_iota(jnp.int32, a_blk.shape, 1)
                    a_s[r0 + t0:r0 + t1, h * HEAD_DIM:h * HEAD_DIM + t1] = (
                        jnp.where(ti >= si, a_blk, 0.0).astype(BF16))
                kst = (kc * jnp.exp2(b_last - bc)).astype(BF16)
                u_s[h * nchunk + c] = _dot_tn(vc, kst)

    def hgrn2_recurrence():
        for h in range(HEADS):
            sl = slice(h * HEAD_DIM, (h + 1) * HEAD_DIM)
            st = st_ref[h]
            for c in range(nchunk):
                r0 = c * CHUNK
                rs = slice(r0, r0 + CHUNK)
                o_c = _dot_nt(qin_s[rs, sl], st.astype(BF16))
                rows = []
                for ib in range(nsub):
                    t0, t1 = ib * SUB, (ib + 1) * SUB
                    a_blk = a_s[r0 + t0:r0 + t1, h * HEAD_DIM:h * HEAD_DIM + t1]
                    rows.append(o_c[t0:t1] + jnp.dot(a_blk, v_s[r0:r0 + t1, sl],
                                                     preferred_element_type=F32))
                hg_s[rs, sl] = jnp.concatenate(rows, axis=0)
                st = dec_s[c:c + 1, sl] * st + u_s[h * nchunk + c]
            st_ref[h] = st

    def rglru_scan():
        row8d = lax.broadcasted_iota(jnp.int32, (SLAB, d), 0)
        hc = h_ref[...]
        for i in range(tm // SLAB):
            a8 = la_s[i * SLAB:(i + 1) * SLAB, :]
            b8 = lb_s[i * SLAB:(i + 1) * SLAB, :]
            for s in (1, 2, 4):
                m = row8d >= s
                b8 = b8 + a8 * jnp.where(m, pltpu.roll(b8, s, 0), 0.0)
                a8 = a8 * jnp.where(m, pltpu.roll(a8, s, 0), 1.0)
            h8 = a8 * hc + b8
            hc = h8[SLAB - 1:SLAB]
            hs_s[i * SLAB:(i + 1) * SLAB, :] = h8
        h_ref[...] = hc

    def merge_out():
        onorm = onorm_ref[...]
        oa = jnp.concatenate(
            [_head_norm_gate(hg_s[:, h * HEAD_DIM:(h + 1) * HEAD_DIM],
                             og_s[:, h * HEAD_DIM:(h + 1) * HEAD_DIM], onorm)
             for h in range(HEADS)], axis=-1).astype(BF16)
        hb = (ly_s[...] * hs_s[...]).astype(BF16)
        y_ref[0] = _merge_out(x_ref[0], oa, hb, ga_s[...], gb_s[...], g_post_ref[...],
                              w_pa, w_pb, w_o)

    project_recurrence_inputs()
    rglru_gates()
    project_gates()
    hgrn2_scores()
    hgrn2_recurrence()
    rglru_scan()
    merge_out()

    @pl.when(j == pl.num_programs(1) - 1)
    def _():
        for h in range(HEADS):
            s_out_ref[0, h] = st_ref[h].T
        h_out_ref[0] = h_ref[...]
        c_out_ref[0] = xp_ref[5:8, :]


def _full(shape):
    return pl.BlockSpec(shape, lambda *_: (0,) * len(shape))


def _prompt_mixer(x, p, tm):
    nb, t, d = x.shape
    assert t % tm == 0 and tm % CHUNK == 0 and d == HEADS * HEAD_DIM and d % PROJ_COLS == 0
    assert tm // CHUNK <= SLAB
    kern = functools.partial(_prompt_mixer_kernel, tm=tm)
    in_specs = [
        pl.BlockSpec((1, tm, d), lambda s, j: (s, j, 0)),
        _full((1, d)), _full((1, d)), _full(p["lb_logits"].shape),
        _full((1, HEAD_DIM)), _full((CONV_W, d)), _full((1, d)),
        _full(p["wa"].shape), _full((1, d)), _full(p["wx"].shape), _full((1, d)), _full((1, d)),
    ]
    weights = []
    for name in ("w_in", "w_pa", "w_pb", "w_o"):
        specs, ops = _col_specs(p[name], PROJ_COLS)
        in_specs += specs
        weights += ops
    out_shape = (
        jax.ShapeDtypeStruct((nb, t, d), F32),
        jax.ShapeDtypeStruct((nb, HEADS, HEAD_DIM, HEAD_DIM), F32),
        jax.ShapeDtypeStruct((nb, 1, d), F32),
        jax.ShapeDtypeStruct((nb, CONV_W - 1, d), F32),
    )
    out_specs = (
        pl.BlockSpec((1, tm, d), lambda s, j: (s, j, 0)),
        pl.BlockSpec((1, HEADS, HEAD_DIM, HEAD_DIM), lambda s, j: (s, 0, 0, 0)),
        pl.BlockSpec((1, 1, d), lambda s, j: (s, 0, 0)),
        pl.BlockSpec((1, CONV_W - 1, d), lambda s, j: (s, 0, 0)),
    )
    tile_f32 = pltpu.VMEM((tm, d), F32)
    tile_bf16 = pltpu.VMEM((tm, d), BF16)
    scratch = [
        pltpu.VMEM((HEADS, HEAD_DIM, HEAD_DIM), F32),
        pltpu.VMEM((1, d), F32),
        pltpu.VMEM((tm + SLAB, d), F32),
        tile_bf16,
        tile_f32, tile_f32, tile_f32,
        tile_bf16,
        tile_f32, tile_f32, tile_f32, tile_f32,
        tile_f32,
        tile_f32,
        tile_bf16,
        tile_bf16,
        pltpu.VMEM((HEADS * (tm // CHUNK), HEAD_DIM, HEAD_DIM), F32),
        pltpu.VMEM((SLAB, d), F32),
        tile_f32, tile_f32,
    ]
    return pl.pallas_call(
        kern, grid=(nb, t // tm), in_specs=in_specs, out_specs=out_specs, out_shape=out_shape,
        scratch_shapes=scratch, name="prompt_mixer",
        compiler_params=pltpu.CompilerParams(
            dimension_semantics=("arbitrary", "arbitrary"),
            vmem_limit_bytes=V7X_VMEM_LIMIT_BYTES),
    )(x, p["g_mix_pre"], p["g_mix_post"], p["lb_logits"], p["onorm"], p["conv_w"],
      p["conv_b"], p["wa"], p["ba"], p["wx"], p["bx"], p["log_lambda"], *weights)


def _ffn_kernel(*refs, nf):
    x_ref, g_pre_ref, g_post_ref = refs[:3]
    w_gate, w_up, w_fo = refs[3:3 + nf], refs[3 + nf:3 + 2 * nf], refs[3 + 2 * nf:-1]
    refs[-1][...] = _ffn(x_ref[...], g_pre_ref[...], g_post_ref[...], w_gate, w_up, w_fo)


def _ffn_call(x, p, tm):
    n, d = x.shape
    assert n % tm == 0
    fi_specs, fi_ops = _col_specs(p["w_fi"], FFN_COLS)
    fo_specs, fo_ops = _col_specs(p["w_fo"], PROJ_COLS)
    return pl.pallas_call(
        functools.partial(_ffn_kernel, nf=len(fi_ops) // 2), grid=(n // tm,),
        in_specs=[pl.BlockSpec((tm, d), lambda i: (i, 0)), _full((1, d)), _full((1, d))]
        + fi_specs + fo_specs,
        out_specs=pl.BlockSpec((tm, d), lambda i: (i, 0)),
        out_shape=jax.ShapeDtypeStruct((n, d), F32), name="swiglu",
        compiler_params=pltpu.CompilerParams(
            dimension_semantics=("arbitrary",), vmem_limit_bytes=V7X_VMEM_LIMIT_BYTES),
    )(x, p["g_ffn_pre"], p["g_ffn_post"], *fi_ops, *fo_ops)


def _sample_token_kernel(
        x_ref, h0_ref, cbuf_ref, g_pre_ref, w_in_ref, lbl_ref, cw_ref, cb_ref,
        wa_ref, ba_ref, wx_ref, bx_ref, ll_ref,
        ft_ref, kt_ref, q_ref, v_ref, og_ref, hb_ref, ga_ref, gb_ref, h_out_ref, c_out_ref):
    d = x_ref.shape[-1]
    xn = _rms(x_ref[...], g_pre_ref[...]).astype(BF16)

    def proj(i):
        return _cdot(xn, [w_in_ref.at[:, i * d:(i + 1) * d]])

    lb = _forget_lower_bound(lbl_ref[...])
    qh = _silu(proj(0))
    f = lb + (1.0 - lb) * jax.nn.sigmoid(proj(1))
    kk = 1.0 - f
    for h in range(HEADS):
        sl = slice(h * HEAD_DIM, (h + 1) * HEAD_DIM)
        ft_ref[sl, :] = f[:, sl].T
        kt_ref[sl, :] = kk[:, sl].T
    q_ref[...] = qh.astype(BF16)
    v_ref[...] = proj(2)
    og_ref[...] = _silu(proj(3))

    lx = proj(4)
    cw = cw_ref[...]
    xc = cb_ref[...] + cbuf_ref[0] * cw[0:1]
    xc = xc + cbuf_ref[1] * cw[1:2]
    xc = xc + cbuf_ref[2] * cw[2:3]
    xc = xc + lx * cw[3:4]
    c_out_ref[0] = cbuf_ref[1]
    c_out_ref[1] = cbuf_ref[2]
    c_out_ref[2] = lx
    a, mult, gate_in = _lru_gates(xc, wa_ref, wx_ref, ba_ref[...], bx_ref[...], ll_ref[...])
    hnew = a * h0_ref[...] + mult * gate_in
    h_out_ref[...] = hnew
    hb_ref[...] = (_gelu_tanh(proj(5)) * hnew).astype(BF16)
    ga_ref[...] = jax.nn.sigmoid(proj(6))
    gb_ref[...] = jax.nn.sigmoid(proj(7))


def _sample_state_kernel(s_ref, ft_ref, kt_ref, q_ref, v_ref, s_out_ref, o_ref, *, bb):
    i = pl.program_id(0)
    lanes = ft_ref.shape[-1]
    shift = lax.rem(lanes - i * bb, lanes)
    ft = pltpu.roll(ft_ref[...], shift, 1)
    kt = pltpu.roll(kt_ref[...], shift, 1)
    for u in range(bb):
        for h in range(HEADS):
            sl = slice(h * HEAD_DIM, (h + 1) * HEAD_DIM)
            s_new = ft[sl, u:u + 1] * s_ref[u, h] + kt[sl, u:u + 1] * v_ref[u:u + 1, sl]
            s_out_ref[u, h] = s_new
            o_ref[u:u + 1, sl] = jnp.dot(q_ref[u:u + 1, sl], s_new.astype(BF16),
                                         preferred_element_type=F32)


def _sample_tail_kernel(
        x_ref, o_ref, og_ref, hb_ref, ga_ref, gb_ref, onorm_ref, g_post_ref, w_pa_ref, w_pb_ref,
        w_o_ref, g_fpre_ref, g_fpost_ref, w_fi_ref, w_fo_ref, y_ref):
    o = o_ref[...]
    og = og_ref[...]
    oa = jnp.concatenate(
        [_head_norm_gate(o[:, h * HEAD_DIM:(h + 1) * HEAD_DIM],
                         og[:, h * HEAD_DIM:(h + 1) * HEAD_DIM], onorm_ref[...])
         for h in range(HEADS)], axis=-1).astype(BF16)
    x1 = _merge_out(x_ref[...], oa, hb_ref[...], ga_ref[...], gb_ref[...], g_post_ref[...],
                    [w_pa_ref], [w_pb_ref], [w_o_ref])
    d_ff = w_fo_ref.shape[0]
    w_fi = _col_refs(w_fi_ref, d_ff)
    y_ref[...] = _ffn(x1, g_fpre_ref[...], g_fpost_ref[...], w_fi[:1], w_fi[1:], [w_fo_ref])


def _sample_step(x, s0, h0, cbuf, p, bb=16):
    nb, d = x.shape
    assert nb == HEAD_DIM and nb % bb == 0
    cp = pltpu.CompilerParams(vmem_limit_bytes=V7X_VMEM_LIMIT_BYTES)
    tok = jax.ShapeDtypeStruct((nb, d), F32)
    tok16 = jax.ShapeDtypeStruct((nb, d), BF16)
    col = jax.ShapeDtypeStruct((d, nb), F32)
    ft, kt, q, v, og, hb, ga, gb, h_new, c_new = pl.pallas_call(
        _sample_token_kernel,
        out_shape=(col, col, tok16, tok, tok, tok16, tok, tok, tok,
                   jax.ShapeDtypeStruct((CONV_W - 1, nb, d), F32)),
        name="sample_token", compiler_params=cp,
    )(x, h0, cbuf, p["g_mix_pre"], p["w_in"], p["lb_logits"], p["conv_w"], p["conv_b"],
      p["wa"], p["ba"], p["wx"], p["bx"], p["log_lambda"])

    s_new, o = pl.pallas_call(
        functools.partial(_sample_state_kernel, bb=bb), grid=(nb // bb,),
        in_specs=[pl.BlockSpec((bb, HEADS, HEAD_DIM, HEAD_DIM), lambda i: (i, 0, 0, 0)),
                  _full((d, nb)), _full((d, nb)),
                  pl.BlockSpec((bb, d), lambda i: (i, 0)), pl.BlockSpec((bb, d), lambda i: (i, 0))],
        out_specs=(pl.BlockSpec((bb, HEADS, HEAD_DIM, HEAD_DIM), lambda i: (i, 0, 0, 0)),
                   pl.BlockSpec((bb, d), lambda i: (i, 0))),
        out_shape=(jax.ShapeDtypeStruct(s0.shape, F32), tok),
        name="sample_state",
        compiler_params=pltpu.CompilerParams(
            dimension_semantics=("arbitrary",), vmem_limit_bytes=V7X_VMEM_LIMIT_BYTES),
    )(s0, ft, kt, q, v)

    y = pl.pallas_call(
        _sample_tail_kernel, out_shape=tok, name="sample_tail", compiler_params=cp,
    )(x, o, og, hb, ga, gb, p["onorm"], p["g_mix_post"], p["w_pa"], p["w_pb"], p["w_o"],
      p["g_ffn_pre"], p["g_ffn_post"], p["w_fi"], p["w_fo"])
    return y, s_new, h_new, c_new


def kernel(x_prompt, x_sample, state_hgrn, state_lru, state_conv, norm_mix_pre, norm_mix_post,
           norm_ffn_pre, norm_ffn_post, w_in, hg_lb_logits, hg_out_norm, lru_conv_w, lru_conv_b,
           lru_w_a, lru_b_a, lru_w_x, lru_b_x, lru_log_lambda, w_branch_a, w_branch_b, w_out,
           w_ffn_in, w_ffn_out):
    depth = w_in.shape[0]
    assert depth == 1 and hg_lb_logits.shape[0] == 2
    nb, t, d = x_prompt.shape
    nbs = x_sample.shape[0]
    assert x_sample.shape[1] == 1

    def row(a):
        return a.reshape(1, -1).astype(F32)

    p = dict(
        g_mix_pre=row(norm_mix_pre[0]), g_mix_post=row(norm_mix_post[0]),
        g_ffn_pre=row(norm_ffn_pre[0]), g_ffn_post=row(norm_ffn_post[0]),
        w_in=w_in[0].astype(BF16), lb_logits=hg_lb_logits.astype(F32),
        onorm=row(hg_out_norm[0]), conv_w=lru_conv_w[0].astype(F32), conv_b=row(lru_conv_b[0]),
        wa=lru_w_a[0].astype(BF16), ba=row(lru_b_a[0]), wx=lru_w_x[0].astype(BF16),
        bx=row(lru_b_x[0]), log_lambda=row(lru_log_lambda[0]),
        w_pa=w_branch_a[0].astype(BF16), w_pb=w_branch_b[0].astype(BF16),
        w_o=w_out[0].astype(BF16), w_fi=w_ffn_in[0].astype(BF16), w_fo=w_ffn_out[0].astype(BF16),
    )

    x1, s_p, h_p, c_p = _prompt_mixer(x_prompt, p, tm=256)
    y_p = _ffn_call(x1.reshape(nb * t, d), p, tm=1024).reshape(nb, t, d)

    y_s, s_s, h_s, c_s = _sample_step(
        x_sample.reshape(nbs, d), state_hgrn[0], state_lru[0],
        jnp.swapaxes(state_conv[0], 0, 1), p)

    return (y_p, y_s.reshape(nbs, 1, d), s_p[None], h_p.reshape(1, nb, d), c_p[None],
            s_s[None], h_s[None], jnp.swapaxes(c_s, 0, 1)[None])
```

```python
import functools

import jax
import jax.numpy as jnp
from jax import lax
from jax.experimental import pallas as pl
from jax.experimental.pallas import tpu as pltpu

F32 = jnp.float32
BF16 = jnp.bfloat16

HEADS = 8
HEAD_DIM = 128
LRU_BLOCKS = 8
LRU_BLOCK_W = 128
CONV_W = 4
LRU_C = 8.0
EPS = 1e-6
CHUNK = 64
SUB = 16
SLAB = 8
PROJ_COLS = 512
FFN_COLS = 256
FFN_ROWS = 256
V7X_VMEM_LIMIT_BYTES = 56 * 1024 * 1024


def _rms(x, g):
    ms = jnp.mean(x * x, axis=-1, keepdims=True)
    return x * lax.rsqrt(ms + EPS) * g


def _dot_nt(a, b):
    return lax.dot_general(a, b, (((1,), (1,)), ((), ())), preferred_element_type=F32)


def _dot_tn(a, b):
    return lax.dot_general(a, b, (((0,), (0,)), ((), ())), preferred_element_type=F32)


def _silu(x):
    return x * jax.nn.sigmoid(x)


def _gelu_tanh(x):
    c = 0.7978845608028654
    return x * (0.5 * (1.0 + jnp.tanh(c * (x + 0.044715 * (x * x * x)))))


def _softplus(z):
    return jnp.maximum(z, 0.0) + jnp.log1p(jnp.exp(-jnp.abs(z)))


def _forget_lower_bound(logits):
    m = jnp.max(logits, axis=0, keepdims=True)
    e = jnp.exp(logits - m)
    return e[0:1] / (e[0:1] + e[1:2])


def _lru_gates(xc, wa_ref, wx_ref, ba, bx, log_lambda):
    ra, ix = [], []
    for n in range(LRU_BLOCKS):
        sl = slice(n * LRU_BLOCK_W, (n + 1) * LRU_BLOCK_W)
        xb = xc[:, sl].astype(BF16)
        ra.append(jnp.dot(xb, wa_ref[n], preferred_element_type=F32))
        ix.append(jnp.dot(xb, wx_ref[n], preferred_element_type=F32))
    ra = jnp.concatenate(ra, axis=-1) + ba
    ix = jnp.concatenate(ix, axis=-1) + bx
    log_a = (-LRU_C) * jax.nn.sigmoid(ra) * _softplus(-log_lambda)
    a = jnp.exp(log_a)
    mult = jnp.sqrt(-jnp.tanh(log_a) * (a * a + 1.0))
    gate_in = jax.nn.sigmoid(ix) * xc
    return a, mult, gate_in


def _head_norm_gate(oh, og_silu, onorm):
    ms = jnp.mean(oh * oh, axis=-1, keepdims=True)
    return oh * lax.rsqrt(ms + EPS) * onorm * og_silu


def _col_refs(w_ref, cols):
    return [w_ref.at[:, c * cols:(c + 1) * cols] for c in range(w_ref.shape[1] // cols)]


def _col_specs(w, cols):
    n = w.shape[1] // cols
    assert n * cols == w.shape[1]
    specs = [pl.BlockSpec((w.shape[0], cols), functools.partial(lambda c, *_: (0, c), c),
                          pipeline_mode=pl.Buffered(1)) for c in range(n)]
    return specs, [w] * n


def _cdot(a, w_chunks):
    parts = [jnp.dot(a, w[...], preferred_element_type=F32) for w in w_chunks]
    return parts[0] if len(parts) == 1 else jnp.concatenate(parts, axis=-1)


def _merge_out(x, oa, hb, ga_sig, gb_sig, g_post, w_pa, w_pb, w_o):
    merged = ga_sig * _cdot(oa, w_pa) + gb_sig * _cdot(hb, w_pb)
    return x + _rms(_cdot(merged.astype(BF16), w_o), g_post)


def _ffn(x, g_pre, g_post, w_gate, w_up, w_fo):
    hn = _rms(x, g_pre).astype(BF16)
    act = jnp.concatenate(
        [(_silu(_cdot(hn, [wg])) * _cdot(hn, [wu])).astype(BF16)
         for wg, wu in zip(w_gate, w_up, strict=True)], axis=-1)
    return x + _rms(_cdot(act, w_fo), g_post)


def _chunk_prefix_sum(b2_ref, r0, sl, row8):
    slabs, carry = [], None
    for i in range(CHUNK // SLAB):
        vv = b2_ref[r0 + i * SLAB:r0 + (i + 1) * SLAB, sl]
        for s in (1, 2, 4):
            vv = vv + jnp.where(row8 >= s, pltpu.roll(vv, s, 0), 0.0)
        if carry is not None:
            vv = vv + carry
        carry = vv[SLAB - 1:SLAB]
        slabs.append(vv)
    return jnp.concatenate(slabs, axis=0)


def _prompt_mixer_kernel(*refs, tm):
    (x_ref, g_pre_ref, g_post_ref, lbl_ref, onorm_ref, cw_ref, cb_ref,
     wa_ref, ba_ref, wx_ref, bx_ref, ll_ref) = refs[:12]
    d = x_ref.shape[-1]
    per = d // PROJ_COLS
    weights = refs[12:12 + 11 * per]
    w_in, w_pa, w_pb, w_o = (weights[:8 * per], weights[8 * per:9 * per],
                             weights[9 * per:10 * per], weights[10 * per:])
    (y_ref, s_out_ref, h_out_ref, c_out_ref,
     st_ref, h_ref, xp_ref, xn_s, qh_s, b2_s, kk_s, v_s, og_s, ly_s, ga_s, gb_s, hg_s, hs_s,
     qin_s, a_s, u_s, dec_s, la_s, lb_s) = refs[12 + 11 * per:]
    j = pl.program_id(1)
    nchunk = tm // CHUNK
    nsub = CHUNK // SUB

    @pl.when(j == 0)
    def _():
        st_ref[...] = jnp.zeros_like(st_ref)
        h_ref[...] = jnp.zeros_like(h_ref)
        xp_ref[0:SLAB, :] = jnp.zeros((SLAB, d), F32)

    def proj(g, c0):
        return jnp.dot(xn_s[...], w_in[(g * d + c0) // PROJ_COLS][...],
                       preferred_element_type=F32)

    def project_recurrence_inputs():
        xn_s[...] = _rms(x_ref[0], g_pre_ref[...]).astype(BF16)
        for c0 in range(0, d, PROJ_COLS):
            xp_ref[SLAB:SLAB + tm, c0:c0 + PROJ_COLS] = proj(4, c0)
        lb = _forget_lower_bound(lbl_ref[...])
        for c0 in range(0, d, PROJ_COLS):
            cs = slice(c0, c0 + PROJ_COLS)
            qh_s[:, cs] = _silu(proj(0, c0))
            f = lb[:, cs] + (1.0 - lb[:, cs]) * jax.nn.sigmoid(proj(1, c0))
            b2_s[:, cs] = jnp.log2(f)
            kk_s[:, cs] = 1.0 - f
            v_s[:, cs] = proj(2, c0).astype(BF16)

    def rglru_gates():
        cw = cw_ref[...]
        xc = cb_ref[...] + xp_ref[5:5 + tm, :] * cw[0:1]
        xc = xc + xp_ref[6:6 + tm, :] * cw[1:2]
        xc = xc + xp_ref[7:7 + tm, :] * cw[2:3]
        xc = xc + xp_ref[8:8 + tm, :] * cw[3:4]
        xp_ref[0:SLAB, :] = xp_ref[tm:tm + SLAB, :]
        a, mult, gate_in = _lru_gates(xc, wa_ref, wx_ref, ba_ref[...], bx_ref[...], ll_ref[...])
        row = lax.broadcasted_iota(jnp.int32, (tm, d), 0)
        mult = jnp.where(jnp.logical_and(row == 0, j == 0), 1.0, mult)
        la_s[...] = a
        lb_s[...] = mult * gate_in

    def project_gates():
        for g, ref, fn in ((7, gb_s, jax.nn.sigmoid), (6, ga_s, jax.nn.sigmoid),
                           (3, og_s, _silu), (5, ly_s, _gelu_tanh)):
            for c0 in range(0, d, PROJ_COLS):
                ref[:, c0:c0 + PROJ_COLS] = fn(proj(g, c0))

    def hgrn2_scores():
        row8 = lax.broadcasted_iota(jnp.int32, (SLAB, HEAD_DIM), 0)
        for h in range(HEADS):
            sl = slice(h * HEAD_DIM, (h + 1) * HEAD_DIM)
            for c in range(nchunk):
                r0 = c * CHUNK
                rs = slice(r0, r0 + CHUNK)
                bc = _chunk_prefix_sum(b2_s, r0, sl, row8)
                qc, kc, vc = qh_s[rs, sl], kk_s[rs, sl], v_s[rs, sl]
                b_last = bc[CHUNK - 1:CHUNK]
                qin_s[rs, sl] = (qc * jnp.exp2(bc)).astype(BF16)
                dec_s[c:c + 1, sl] = jnp.exp2(b_last)
                for ib in range(nsub):
                    t0, t1 = ib * SUB, (ib + 1) * SUB
                    ref_b = bc[t0 + SUB // 2 - 1:t0 + SUB // 2]
                    qs = (qc[t0:t1] * jnp.exp2(bc[t0:t1] - ref_b)).astype(BF16)
                    ks = (kc[:t1] * jnp.exp2(ref_b - bc[:t1])).astype(BF16)
                    a_blk = _dot_nt(qs, ks)
                    ti = lax.broadcasted_iota(jnp.int32, a_blk.shape, 0) + t0
                    si = lax.broadcasted_iota(jnp.int32, a_blk.shape, 1)
                    a_s[r0 + t0:r0 + t1, h * HEAD_DIM:h * HEAD_DIM + t1] = (
                        jnp.where(ti >= si, a_blk, 0.0).astype(BF16))
                kst = (kc * jnp.exp2(b_last - bc)).astype(BF16)
                u_s[h * nchunk + c] = _dot_tn(vc, kst)

    def hgrn2_recurrence():
        for h in range(HEADS):
            sl = slice(h * HEAD_DIM, (h + 1) * HEAD_DIM)
            st = st_ref[h]
            for c in range(nchunk):
                r0 = c * CHUNK
                rs = slice(r0, r0 + CHUNK)
                o_c = _dot_nt(qin_s[rs, sl], st.astype(BF16))
                rows = []
                for ib in range(nsub):
                    t0, t1 = ib * SUB, (ib + 1) * SUB
                    a_blk = a_s[r0 + t0:r0 + t1, h * HEAD_DIM:h * HEAD_DIM + t1]
                    rows.append(o_c[t0:t1] + jnp.dot(a_blk, v_s[r0:r0 + t1, sl],
                                                     preferred_element_type=F32))
                hg_s[rs, sl] = jnp.concatenate(rows, axis=0)
                st = dec_s[c:c + 1, sl] * st + u_s[h * nchunk + c]
            st_ref[h] = st

    def rglru_scan():
        row8d = lax.broadcasted_iota(jnp.int32, (SLAB, d), 0)
        hc = h_ref[...]
        for i in range(tm // SLAB):
            a8 = la_s[i * SLAB:(i + 1) * SLAB, :]
            b8 = lb_s[i * SLAB:(i + 1) * SLAB, :]
            for s in (1, 2, 4):
                m = row8d >= s
                b8 = b8 + a8 * jnp.where(m, pltpu.roll(b8, s, 0), 0.0)
                a8 = a8 * jnp.where(m, pltpu.roll(a8, s, 0), 1.0)
            h8 = a8 * hc + b8
            hc = h8[SLAB - 1:SLAB]
            hs_s[i * SLAB:(i + 1) * SLAB, :] = h8
        h_ref[...] = hc

    def merge_out():
        onorm = onorm_ref[...]
        oa = jnp.concatenate(
            [_head_norm_gate(hg_s[:, h * HEAD_DIM:(h + 1) * HEAD_DIM],
                             og_s[:, h * HEAD_DIM:(h + 1) * HEAD_DIM], onorm)
             for h in range(HEADS)], axis=-1).astype(BF16)
        hb = (ly_s[...] * hs_s[...]).astype(BF16)
        y_ref[0] = _merge_out(x_ref[0], oa, hb, ga_s[...], gb_s[...], g_post_ref[...],
                              w_pa, w_pb, w_o)

    project_recurrence_inputs()
    rglru_gates()
    project_gates()
    hgrn2_scores()
    hgrn2_recurrence()
    rglru_scan()
    merge_out()

    @pl.when(j == pl.num_programs(1) - 1)
    def _():
        for h in range(HEADS):
            s_out_ref[0, h] = st_ref[h].T
        h_out_ref[0] = h_ref[...]
        c_out_ref[0] = xp_ref[5:8, :]


def _full(shape):
    return pl.BlockSpec(shape, lambda *_: (0,) * len(shape))


def _prompt_mixer(x, p, tm):
    nb, t, d = x.shape
    assert t % tm == 0 and tm % CHUNK == 0 and d == HEADS * HEAD_DIM and d % PROJ_COLS == 0
    assert tm // CHUNK <= SLAB
    kern = functools.partial(_prompt_mixer_kernel, tm=tm)
    in_specs = [
        pl.BlockSpec((1, tm, d), lambda s, j: (s, j, 0)),
        _full((1, d)), _full((1, d)), _full(p["lb_logits"].shape),
        _full((1, HEAD_DIM)), _full((CONV_W, d)), _full((1, d)),
        _full(p["wa"].shape), _full((1, d)), _full(p["wx"].shape), _full((1, d)), _full((1, d)),
    ]
    weights = []
    for name in ("w_in", "w_pa", "w_pb", "w_o"):
        specs, ops = _col_specs(p[name], PROJ_COLS)
        in_specs += specs
        weights += ops
    out_shape = (
        jax.ShapeDtypeStruct((nb, t, d), F32),
        jax.ShapeDtypeStruct((nb, HEADS, HEAD_DIM, HEAD_DIM), F32),
        jax.ShapeDtypeStruct((nb, 1, d), F32),
        jax.ShapeDtypeStruct((nb, CONV_W - 1, d), F32),
    )
    out_specs = (
        pl.BlockSpec((1, tm, d), lambda s, j: (s, j, 0)),
        pl.BlockSpec((1, HEADS, HEAD_DIM, HEAD_DIM), lambda s, j: (s, 0, 0, 0)),
        pl.BlockSpec((1, 1, d), lambda s, j: (s, 0, 0)),
        pl.BlockSpec((1, CONV_W - 1, d), lambda s, j: (s, 0, 0)),
    )
    tile_f32 = pltpu.VMEM((tm, d), F32)
    tile_bf16 = pltpu.VMEM((tm, d), BF16)
    scratch = [
        pltpu.VMEM((HEADS, HEAD_DIM, HEAD_DIM), F32),
        pltpu.VMEM((1, d), F32),
        pltpu.VMEM((tm + SLAB, d), F32),
        tile_bf16,
        tile_f32, tile_f32, tile_f32,
        tile_bf16,
        tile_f32, tile_f32, tile_f32, tile_f32,
        tile_f32,
        tile_f32,
        tile_bf16,
        tile_bf16,
        pltpu.VMEM((HEADS * (tm // CHUNK), HEAD_DIM, HEAD_DIM), F32),
        pltpu.VMEM((SLAB, d), F32),
        tile_f32, tile_f32,
    ]
    return pl.pallas_call(
        kern, grid=(nb, t // tm), in_specs=in_specs, out_specs=out_specs, out_shape=out_shape,
        scratch_shapes=scratch, name="prompt_mixer",
        compiler_params=pltpu.CompilerParams(
            dimension_semantics=("arbitrary", "arbitrary"),
            vmem_limit_bytes=V7X_VMEM_LIMIT_BYTES),
    )(x, p["g_mix_pre"], p["g_mix_post"], p["lb_logits"], p["onorm"], p["conv_w"],
      p["conv_b"], p["wa"], p["ba"], p["wx"], p["bx"], p["log_lambda"], *weights)


def _ffn_kernel(*refs, nf):
    x_ref, g_pre_ref, g_post_ref = refs[:3]
    w_gate, w_up, w_fo = refs[3:3 + nf], refs[3 + nf:3 + 2 * nf], refs[3 + 2 * nf:-1]
    for r0 in range(0, x_ref.shape[0], FFN_ROWS):
        rows = slice(r0, r0 + FFN_ROWS)
        refs[-1][rows, :] = _ffn(x_ref[rows, :], g_pre_ref[...], g_post_ref[...],
                                 w_gate, w_up, w_fo)


def _ffn_call(x, p, tm):
    n, d = x.shape
    assert n % tm == 0 and tm % FFN_ROWS == 0
    fi_specs, fi_ops = _col_specs(p["w_fi"], FFN_COLS)
    fo_specs, fo_ops = _col_specs(p["w_fo"], PROJ_COLS)
    return pl.pallas_call(
        functools.partial(_ffn_kernel, nf=len(fi_ops) // 2), grid=(n // tm,),
        in_specs=[pl.BlockSpec((tm, d), lambda i: (i, 0)), _full((1, d)), _full((1, d))]
        + fi_specs + fo_specs,
        out_specs=pl.BlockSpec((tm, d), lambda i: (i, 0)),
        out_shape=jax.ShapeDtypeStruct((n, d), F32), name="swiglu",
        compiler_params=pltpu.CompilerParams(
            dimension_semantics=("arbitrary",), vmem_limit_bytes=V7X_VMEM_LIMIT_BYTES),
    )(x, p["g_ffn_pre"], p["g_ffn_post"], *fi_ops, *fo_ops)


def _sample_token_kernel(
        x_ref, h0_ref, cbuf_ref, g_pre_ref, w_in_ref, lbl_ref, cw_ref, cb_ref,
        wa_ref, ba_ref, wx_ref, bx_ref, ll_ref,
        ft_ref, kt_ref, q_ref, v_ref, og_ref, hb_ref, ga_ref, gb_ref, h_out_ref, c_out_ref):
    d = x_ref.shape[-1]
    xn = _rms(x_ref[...], g_pre_ref[...]).astype(BF16)

    def proj(i):
        return _cdot(xn, [w_in_ref.at[:, i * d:(i + 1) * d]])

    lb = _forget_lower_bound(lbl_ref[...])
    qh = _silu(proj(0))
    f = lb + (1.0 - lb) * jax.nn.sigmoid(proj(1))
    kk = 1.0 - f
    for h in range(HEADS):
        sl = slice(h * HEAD_DIM, (h + 1) * HEAD_DIM)
        ft_ref[sl, :] = f[:, sl].T
        kt_ref[sl, :] = kk[:, sl].T
    q_ref[...] = qh.astype(BF16)
    v_ref[...] = proj(2)
    og_ref[...] = _silu(proj(3))

    lx = proj(4)
    cw = cw_ref[...]
    xc = cb_ref[...] + cbuf_ref[0] * cw[0:1]
    xc = xc + cbuf_ref[1] * cw[1:2]
    xc = xc + cbuf_ref[2] * cw[2:3]
    xc = xc + lx * cw[3:4]
    c_out_ref[0] = cbuf_ref[1]
    c_out_ref[1] = cbuf_ref[2]
    c_out_ref[2] = lx
    a, mult, gate_in = _lru_gates(xc, wa_ref, wx_ref, ba_ref[...], bx_ref[...], ll_ref[...])
    hnew = a * h0_ref[...] + mult * gate_in
    h_out_ref[...] = hnew
    hb_ref[...] = (_gelu_tanh(proj(5)) * hnew).astype(BF16)
    ga_ref[...] = jax.nn.sigmoid(proj(6))
    gb_ref[...] = jax.nn.sigmoid(proj(7))


def _sample_state_kernel(s_ref, ft_ref, kt_ref, q_ref, v_ref, s_out_ref, o_ref, *, bb):
    i = pl.program_id(0)
    lanes = ft_ref.shape[-1]
    shift = lax.rem(lanes - i * bb, lanes)
    ft = pltpu.roll(ft_ref[...], shift, 1)
    kt = pltpu.roll(kt_ref[...], shift, 1)
    for u in range(bb):
        for h in range(HEADS):
            sl = slice(h * HEAD_DIM, (h + 1) * HEAD_DIM)
            s_new = ft[sl, u:u + 1] * s_ref[u, h] + kt[sl, u:u + 1] * v_ref[u:u + 1, sl]
            s_out_ref[u, h] = s_new
            o_ref[u:u + 1, sl] = jnp.dot(q_ref[u:u + 1, sl], s_new.astype(BF16),
                                         preferred_element_type=F32)


def _sample_tail_kernel(
        x_ref, o_ref, og_ref, hb_ref, ga_ref, gb_ref, onorm_ref, g_post_ref, w_pa_ref, w_pb_ref,
        w_o_ref, g_fpre_ref, g_fpost_ref, w_fi_ref, w_fo_ref, y_ref):
    o = o_ref[...]
    og = og_ref[...]
    oa = jnp.concatenate(
        [_head_norm_gate(o[:, h * HEAD_DIM:(h + 1) * HEAD_DIM],
                         og[:, h * HEAD_DIM:(h + 1) * HEAD_DIM], onorm_ref[...])
         for h in range(HEADS)], axis=-1).astype(BF16)
    x1 = _merge_out(x_ref[...], oa, hb_ref[...], ga_ref[...], gb_ref[...], g_post_ref[...],
                    [w_pa_ref], [w_pb_ref], [w_o_ref])
    d_ff = w_fo_ref.shape[0]
    w_fi = _col_refs(w_fi_ref, d_ff)
    y_ref[...] = _ffn(x1, g_fpre_ref[...], g_fpost_ref[...], w_fi[:1], w_fi[1:], [w_fo_ref])


def _sample_step(x, s0, h0, cbuf, p, bb=16):
    nb, d = x.shape
    assert nb == HEAD_DIM and nb % bb == 0
    cp = pltpu.CompilerParams(vmem_limit_bytes=V7X_VMEM_LIMIT_BYTES)
    tok = jax.ShapeDtypeStruct((nb, d), F32)
    tok16 = jax.ShapeDtypeStruct((nb, d), BF16)
    col = jax.ShapeDtypeStruct((d, nb), F32)
    ft, kt, q, v, og, hb, ga, gb, h_new, c_new = pl.pallas_call(
        _sample_token_kernel,
        out_shape=(col, col, tok16, tok, tok, tok16, tok, tok, tok,
                   jax.ShapeDtypeStruct((CONV_W - 1, nb, d), F32)),
        name="sample_token", compiler_params=cp,
    )(x, h0, cbuf, p["g_mix_pre"], p["w_in"], p["lb_logits"], p["conv_w"], p["conv_b"],
      p["wa"], p["ba"], p["wx"], p["bx"], p["log_lambda"])

    s_new, o = pl.pallas_call(
        functools.partial(_sample_state_kernel, bb=bb), grid=(nb // bb,),
        in_specs=[pl.BlockSpec((bb, HEADS, HEAD_DIM, HEAD_DIM), lambda i: (i, 0, 0, 0)),
                  _full((d, nb)), _full((d, nb)),
                  pl.BlockSpec((bb, d), lambda i: (i, 0)), pl.BlockSpec((bb, d), lambda i: (i, 0))],
        out_specs=(pl.BlockSpec((bb, HEADS, HEAD_DIM, HEAD_DIM), lambda i: (i, 0, 0, 0)),
                   pl.BlockSpec((bb, d), lambda i: (i, 0))),
        out_shape=(jax.ShapeDtypeStruct(s0.shape, F32), tok),
        name="sample_state",
        compiler_params=pltpu.CompilerParams(
            dimension_semantics=("arbitrary",), vmem_limit_bytes=V7X_VMEM_LIMIT_BYTES),
    )(s0, ft, kt, q, v)

    y = pl.pallas_call(
        _sample_tail_kernel, out_shape=tok, name="sample_tail", compiler_params=cp,
    )(x, o, og, hb, ga, gb, p["onorm"], p["g_mix_post"], p["w_pa"], p["w_pb"], p["w_o"],
      p["g_ffn_pre"], p["g_ffn_post"], p["w_fi"], p["w_fo"])
    return y, s_new, h_new, c_new


def kernel(x_prompt, x_sample, state_hgrn, state_lru, state_conv, norm_mix_pre, norm_mix_post,
           norm_ffn_pre, norm_ffn_post, w_in, hg_lb_logits, hg_out_norm, lru_conv_w, lru_conv_b,
           lru_w_a, lru_b_a, lru_w_x, lru_b_x, lru_log_lambda, w_branch_a, w_branch_b, w_out,
           w_ffn_in, w_ffn_out):
    depth = w_in.shape[0]
    assert depth == 1 and hg_lb_logits.shape[0] == 2
    nb, t, d = x_prompt.shape
    nbs = x_sample.shape[0]
    assert x_sample.shape[1] == 1

    def row(a):
        return a.reshape(1, -1).astype(F32)

    p = dict(
        g_mix_pre=row(norm_mix_pre[0]), g_mix_post=row(norm_mix_post[0]),
        g_ffn_pre=row(norm_ffn_pre[0]), g_ffn_post=row(norm_ffn_post[0]),
        w_in=w_in[0].astype(BF16), lb_logits=hg_lb_logits.astype(F32),
        onorm=row(hg_out_norm[0]), conv_w=lru_conv_w[0].astype(F32), conv_b=row(lru_conv_b[0]),
        wa=lru_w_a[0].astype(BF16), ba=row(lru_b_a[0]), wx=lru_w_x[0].astype(BF16),
        bx=row(lru_b_x[0]), log_lambda=row(lru_log_lambda[0]),
        w_pa=w_branch_a[0].astype(BF16), w_pb=w_branch_b[0].astype(BF16),
        w_o=w_out[0].astype(BF16), w_fi=w_ffn_in[0].astype(BF16), w_fo=w_ffn_out[0].astype(BF16),
    )

    x1, s_p, h_p, c_p = _prompt_mixer(x_prompt, p, tm=256)
    y_p = _ffn_call(x1.reshape(nb * t, d), p, tm=1024).reshape(nb, t, d)

    y_s, s_s, h_s, c_s = _sample_step(
        x_sample.reshape(nbs, d), state_hgrn[0], state_lru[0],
        jnp.swapaxes(state_conv[0], 0, 1), p)

    return (y_p, y_s.reshape(nbs, 1, d), s_p[None], h_p.reshape(1, nb, d), c_p[None],
            s_s[None], h_s[None], jnp.swapaxes(c_s, 0, 1)[None])
```

```python
import functools

import jax
import jax.numpy as jnp
from jax import lax
from jax.experimental import pallas as pl
from jax.experimental.pallas import tpu as pltpu

F32 = jnp.float32
BF16 = jnp.bfloat16

HEADS = 8
HEAD_DIM = 128
LRU_BLOCKS = 8
LRU_BLOCK_W = 128
CONV_W = 4
LRU_C = 8.0
EPS = 1e-6
CHUNK = 64
SUB = 16
SLAB = 8
SUBTILES = 2
PROJ_COLS = 512
FFN_COLS = 256
V7X_VMEM_LIMIT_BYTES = 56 * 1024 * 1024


def _rms(x, g):
    ms = jnp.mean(x * x, axis=-1, keepdims=True)
    return x * lax.rsqrt(ms + EPS) * g


def _dot_nt(a, b):
    return lax.dot_general(a, b, (((1,), (1,)), ((), ())), preferred_element_type=F32)


def _dot_tn(a, b):
    return lax.dot_general(a, b, (((0,), (0,)), ((), ())), preferred_element_type=F32)


def _silu(x):
    return x * jax.nn.sigmoid(x)


def _gelu_tanh(x):
    c = 0.7978845608028654
    return x * (0.5 * (1.0 + jnp.tanh(c * (x + 0.044715 * (x * x * x)))))


def _softplus(z):
    return jnp.maximum(z, 0.0) + jnp.log1p(jnp.exp(-jnp.abs(z)))


def _forget_lower_bound(logits):
    m = jnp.max(logits, axis=0, keepdims=True)
    e = jnp.exp(logits - m)
    return e[0:1] / (e[0:1] + e[1:2])


def _lru_gates(xc, wa_ref, wx_ref, ba, bx, log_lambda):
    ra, ix = [], []
    for n in range(LRU_BLOCKS):
        sl = slice(n * LRU_BLOCK_W, (n + 1) * LRU_BLOCK_W)
        xb = xc[:, sl].astype(BF16)
        ra.append(jnp.dot(xb, wa_ref[n], preferred_element_type=F32))
        ix.append(jnp.dot(xb, wx_ref[n], preferred_element_type=F32))
    ra = jnp.concatenate(ra, axis=-1) + ba
    ix = jnp.concatenate(ix, axis=-1) + bx
    log_a = (-LRU_C) * jax.nn.sigmoid(ra) * _softplus(-log_lambda)
    a = jnp.exp(log_a)
    mult = jnp.sqrt(-jnp.tanh(log_a) * (a * a + 1.0))
    gate_in = jax.nn.sigmoid(ix) * xc
    return a, mult, gate_in


def _head_norm_gate(oh, og_silu, onorm):
    ms = jnp.mean(oh * oh, axis=-1, keepdims=True)
    return oh * lax.rsqrt(ms + EPS) * onorm * og_silu


def _col_refs(w_ref, cols):
    return [w_ref.at[:, c * cols:(c + 1) * cols] for c in range(w_ref.shape[1] // cols)]


def _col_specs(w, cols):
    n = w.shape[1] // cols
    assert n * cols == w.shape[1]
    specs = [pl.BlockSpec((w.shape[0], cols), functools.partial(lambda c, *_: (0, c), c),
                          pipeline_mode=pl.Buffered(1)) for c in range(n)]
    return specs, [w] * n


def _cdot(a, w_chunks):
    parts = [jnp.dot(a, w[...], preferred_element_type=F32) for w in w_chunks]
    return parts[0] if len(parts) == 1 else jnp.concatenate(parts, axis=-1)


def _merge_out(x, oa, hb, ga_sig, gb_sig, g_post, w_pa, w_pb, w_o):
    merged = ga_sig * _cdot(oa, w_pa) + gb_sig * _cdot(hb, w_pb)
    return x + _rms(_cdot(merged.astype(BF16), w_o), g_post)


def _ffn(x, g_pre, g_post, w_gate, w_up, w_fo):
    hn = _rms(x, g_pre).astype(BF16)
    act = jnp.concatenate(
        [(_silu(_cdot(hn, [wg])) * _cdot(hn, [wu])).astype(BF16)
         for wg, wu in zip(w_gate, w_up, strict=True)], axis=-1)
    return x + _rms(_cdot(act, w_fo), g_post)


def _chunk_prefix_sum(b2_ref, r0, sl, row8):
    slabs, carry = [], None
    for i in range(CHUNK // SLAB):
        vv = b2_ref[r0 + i * SLAB:r0 + (i + 1) * SLAB, sl]
        for s in (1, 2, 4):
            vv = vv + jnp.where(row8 >= s, pltpu.roll(vv, s, 0), 0.0)
        if carry is not None:
            vv = vv + carry
        carry = vv[SLAB - 1:SLAB]
        slabs.append(vv)
    return jnp.concatenate(slabs, axis=0)


def _prompt_mixer_kernel(*refs, tm):
    (x_ref, g_pre_ref, g_post_ref, lbl_ref, onorm_ref, cw_ref, cb_ref,
     wa_ref, ba_ref, wx_ref, bx_ref, ll_ref) = refs[:12]
    d = x_ref.shape[-1]
    per = d // PROJ_COLS
    weights = refs[12:12 + 11 * per]
    w_in, w_pa, w_pb, w_o = (weights[:8 * per], weights[8 * per:9 * per],
                             weights[9 * per:10 * per], weights[10 * per:])
    (y_ref, s_out_ref, h_out_ref, c_out_ref,
     st_ref, h_ref, xp_ref, xn_s, qh_s, b2_s, kk_s, v_s, og_s, ly_s, ga_s, gb_s, hg_s, hs_s,
     qin_s, a_s, u_s, dec_s, la_s, lb_s) = refs[12 + 11 * per:]
    j = pl.program_id(1)
    ts = tm // SUBTILES
    nchunk = ts // CHUNK
    nsub = CHUNK // SUB

    @pl.when(j == 0)
    def _():
        st_ref[...] = jnp.zeros_like(st_ref)
        h_ref[...] = jnp.zeros_like(h_ref)
        xp_ref[0:SLAB, :] = jnp.zeros((SLAB, d), F32)

    def sub_tile(base, first):
        rows_all = slice(base, base + ts)

        def proj(g, c0):
            return jnp.dot(xn_s[rows_all, :], w_in[(g * d + c0) // PROJ_COLS][...],
                           preferred_element_type=F32)

        def project_recurrence_inputs():
            xn_s[rows_all, :] = _rms(x_ref[0, rows_all, :], g_pre_ref[...]).astype(BF16)
            for c0 in range(0, d, PROJ_COLS):
                xp_ref[SLAB + base:SLAB + base + ts, c0:c0 + PROJ_COLS] = proj(4, c0)
            lb = _forget_lower_bound(lbl_ref[...])
            for c0 in range(0, d, PROJ_COLS):
                cs = slice(c0, c0 + PROJ_COLS)
                qh_s[rows_all, cs] = _silu(proj(0, c0))
                f = lb[:, cs] + (1.0 - lb[:, cs]) * jax.nn.sigmoid(proj(1, c0))
                b2_s[rows_all, cs] = jnp.log2(f)
                kk_s[rows_all, cs] = 1.0 - f
                v_s[rows_all, cs] = proj(2, c0).astype(BF16)

        def rglru_gates():
            cw = cw_ref[...]
            xc = cb_ref[...] + xp_ref[5 + base:5 + base + ts, :] * cw[0:1]
            xc = xc + xp_ref[6 + base:6 + base + ts, :] * cw[1:2]
            xc = xc + xp_ref[7 + base:7 + base + ts, :] * cw[2:3]
            xc = xc + xp_ref[8 + base:8 + base + ts, :] * cw[3:4]
            a, mult, gate_in = _lru_gates(xc, wa_ref, wx_ref, ba_ref[...], bx_ref[...], ll_ref[...])
            if first:
                row = lax.broadcasted_iota(jnp.int32, (ts, d), 0)
                mult = jnp.where(jnp.logical_and(row == 0, j == 0), 1.0, mult)
            la_s[rows_all, :] = a
            lb_s[rows_all, :] = mult * gate_in

        def project_gates():
            for c0 in range(0, d, PROJ_COLS):
                cs = slice(c0, c0 + PROJ_COLS)
                og_s[rows_all, cs] = _silu(proj(3, c0))
                ly_s[rows_all, cs] = _gelu_tanh(proj(5, c0))
                ga_s[rows_all, cs] = jax.nn.sigmoid(proj(6, c0))
                gb_s[rows_all, cs] = jax.nn.sigmoid(proj(7, c0))

        def hgrn2_scores():
            row8 = lax.broadcasted_iota(jnp.int32, (SLAB, HEAD_DIM), 0)
            for h in range(HEADS):
                sl = slice(h * HEAD_DIM, (h + 1) * HEAD_DIM)
                for c in range(nchunk):
                    r0 = base + c * CHUNK
                    rs = slice(r0, r0 + CHUNK)
                    cc = r0 // CHUNK
                    bc = _chunk_prefix_sum(b2_s, r0, sl, row8)
                    qc, kc, vc = qh_s[rs, sl], kk_s[rs, sl], v_s[rs, sl]
                    b_last = bc[CHUNK - 1:CHUNK]
                    qin_s[rs, sl] = (qc * jnp.exp2(bc)).astype(BF16)
                    dec_s[cc:cc + 1, sl] = jnp.exp2(b_last)
                    for ib in range(nsub):
                        t0, t1 = ib * SUB, (ib + 1) * SUB
                        ref_b = bc[t0 + SUB // 2 - 1:t0 + SUB // 2]
                        qs = (qc[t0:t1] * jnp.exp2(bc[t0:t1] - ref_b)).astype(BF16)
                        ks = (kc[:t1] * jnp.exp2(ref_b - bc[:t1])).astype(BF16)
                        a_blk = _dot_nt(qs, ks)
                        ti = lax.broadcasted_iota(jnp.int32, a_blk.shape, 0) + t0
                        si = lax.broadcasted_iota(jnp.int32, a_blk.shape, 1)
                        a_s[r0 + t0:r0 + t1, h * HEAD_DIM:h * HEAD_DIM + t1] = (
                            jnp.where(ti >= si, a_blk, 0.0).astype(BF16))
                    kst = (kc * jnp.exp2(b_last - bc)).astype(BF16)
                    u_s[h * (tm // CHUNK) + cc] = _dot_tn(vc, kst)

        def hgrn2_recurrence():
            for h in range(HEADS):
                sl = slice(h * HEAD_DIM, (h + 1) * HEAD_DIM)
                st = st_ref[h]
                for c in range(nchunk):
                    r0 = base + c * CHUNK
                    rs = slice(r0, r0 + CHUNK)
                    cc = r0 // CHUNK
                    o_c = _dot_nt(qin_s[rs, sl], st.astype(BF16))
                    rows = []
                    for ib in range(nsub):
                        t0, t1 = ib * SUB, (ib + 1) * SUB
                        a_blk = a_s[r0 + t0:r0 + t1, h * HEAD_DIM:h * HEAD_DIM + t1]
                        rows.append(o_c[t0:t1] + jnp.dot(a_blk, v_s[r0:r0 + t1, sl],
                                                         preferred_element_type=F32))
                    hg_s[rs, sl] = jnp.concatenate(rows, axis=0)
                    st = dec_s[cc:cc + 1, sl] * st + u_s[h * (tm // CHUNK) + cc]
                st_ref[h] = st

        def rglru_scan():
            row8d = lax.broadcasted_iota(jnp.int32, (SLAB, d), 0)
            hc = h_ref[...]
            for i in range(ts // SLAB):
                rr = slice(base + i * SLAB, base + (i + 1) * SLAB)
                a8 = la_s[rr, :]
                b8 = lb_s[rr, :]
                for s in (1, 2, 4):
                    m = row8d >= s
                    b8 = b8 + a8 * jnp.where(m, pltpu.roll(b8, s, 0), 0.0)
                    a8 = a8 * jnp.where(m, pltpu.roll(a8, s, 0), 1.0)
                h8 = a8 * hc + b8
                hc = h8[SLAB - 1:SLAB]
                hs_s[rr, :] = h8
            h_ref[...] = hc

        def merge_out():
            onorm = onorm_ref[...]
            oa = jnp.concatenate(
                [_head_norm_gate(hg_s[rows_all, h * HEAD_DIM:(h + 1) * HEAD_DIM],
                                 og_s[rows_all, h * HEAD_DIM:(h + 1) * HEAD_DIM], onorm)
                 for h in range(HEADS)], axis=-1).astype(BF16)
            hb = (ly_s[rows_all, :] * hs_s[rows_all, :]).astype(BF16)
            y_ref[0, rows_all, :] = _merge_out(
                x_ref[0, rows_all, :], oa, hb, ga_s[rows_all, :], gb_s[rows_all, :],
                g_post_ref[...], w_pa, w_pb, w_o)

        project_recurrence_inputs()
        rglru_gates()
        project_gates()
        hgrn2_scores()
        hgrn2_recurrence()
        rglru_scan()
        merge_out()

    for t in range(SUBTILES):
        sub_tile(t * ts, t == 0)
    xp_ref[0:SLAB, :] = xp_ref[tm:tm + SLAB, :]

    @pl.when(j == pl.num_programs(1) - 1)
    def _():
        for h in range(HEADS):
            s_out_ref[0, h] = st_ref[h].T
        h_out_ref[0] = h_ref[...]
        c_out_ref[0] = xp_ref[5:8, :]


def _full(shape):
    return pl.BlockSpec(shape, lambda *_: (0,) * len(shape))


def _prompt_mixer(x, p, tm):
    nb, t, d = x.shape
    assert t % tm == 0 and tm % CHUNK == 0 and d == HEADS * HEAD_DIM and d % PROJ_COLS == 0
    assert tm // CHUNK <= SLAB and tm % (SUBTILES * CHUNK) == 0
    kern = functools.partial(_prompt_mixer_kernel, tm=tm)
    in_specs = [
        pl.BlockSpec((1, tm, d), lambda s, j: (s, j, 0)),
        _full((1, d)), _full((1, d)), _full(p["lb_logits"].shape),
        _full((1, HEAD_DIM)), _full((CONV_W, d)), _full((1, d)),
        _full(p["wa"].shape), _full((1, d)), _full(p["wx"].shape), _full((1, d)), _full((1, d)),
    ]
    weights = []
    for name in ("w_in", "w_pa", "w_pb", "w_o"):
        specs, ops = _col_specs(p[name], PROJ_COLS)
        in_specs += specs
        weights += ops
    out_shape = (
        jax.ShapeDtypeStruct((nb, t, d), F32),
        jax.ShapeDtypeStruct((nb, HEADS, HEAD_DIM, HEAD_DIM), F32),
        jax.ShapeDtypeStruct((nb, 1, d), F32),
        jax.ShapeDtypeStruct((nb, CONV_W - 1, d), F32),
    )
    out_specs = (
        pl.BlockSpec((1, tm, d), lambda s, j: (s, j, 0)),
        pl.BlockSpec((1, HEADS, HEAD_DIM, HEAD_DIM), lambda s, j: (s, 0, 0, 0)),
        pl.BlockSpec((1, 1, d), lambda s, j: (s, 0, 0)),
        pl.BlockSpec((1, CONV_W - 1, d), lambda s, j: (s, 0, 0)),
    )
    tile_f32 = pltpu.VMEM((tm, d), F32)
    tile_bf16 = pltpu.VMEM((tm, d), BF16)
    scratch = [
        pltpu.VMEM((HEADS, HEAD_DIM, HEAD_DIM), F32),
        pltpu.VMEM((1, d), F32),
        pltpu.VMEM((tm + SLAB, d), F32),
        tile_bf16,
        tile_f32, tile_f32, tile_f32,
        tile_bf16,
        tile_f32, tile_f32, tile_f32, tile_f32,
        tile_f32,
        tile_f32,
        tile_bf16,
        tile_bf16,
        pltpu.VMEM((HEADS * (tm // CHUNK), HEAD_DIM, HEAD_DIM), F32),
        pltpu.VMEM((SLAB, d), F32),
        tile_f32, tile_f32,
    ]
    return pl.pallas_call(
        kern, grid=(nb, t // tm), in_specs=in_specs, out_specs=out_specs, out_shape=out_shape,
        scratch_shapes=scratch, name="prompt_mixer",
        compiler_params=pltpu.CompilerParams(
            dimension_semantics=("arbitrary", "arbitrary"),
            vmem_limit_bytes=V7X_VMEM_LIMIT_BYTES),
    )(x, p["g_mix_pre"], p["g_mix_post"], p["lb_logits"], p["onorm"], p["conv_w"],
      p["conv_b"], p["wa"], p["ba"], p["wx"], p["bx"], p["log_lambda"], *weights)


def _ffn_kernel(*refs, nf):
    x_ref, g_pre_ref, g_post_ref = refs[:3]
    w_gate, w_up, w_fo = refs[3:3 + nf], refs[3 + nf:3 + 2 * nf], refs[3 + 2 * nf:-1]
    refs[-1][...] = _ffn(x_ref[...], g_pre_ref[...], g_post_ref[...], w_gate, w_up, w_fo)


def _ffn_call(x, p, tm):
    n, d = x.shape
    assert n % tm == 0
    fi_specs, fi_ops = _col_specs(p["w_fi"], FFN_COLS)
    fo_specs, fo_ops = _col_specs(p["w_fo"], PROJ_COLS)
    return pl.pallas_call(
        functools.partial(_ffn_kernel, nf=len(fi_ops) // 2), grid=(n // tm,),
        in_specs=[pl.BlockSpec((tm, d), lambda i: (i, 0)), _full((1, d)), _full((1, d))]
        + fi_specs + fo_specs,
        out_specs=pl.BlockSpec((tm, d), lambda i: (i, 0)),
        out_shape=jax.ShapeDtypeStruct((n, d), F32), name="swiglu",
        compiler_params=pltpu.CompilerParams(
            dimension_semantics=("arbitrary",), vmem_limit_bytes=V7X_VMEM_LIMIT_BYTES),
    )(x, p["g_ffn_pre"], p["g_ffn_post"], *fi_ops, *fo_ops)


def _sample_token_kernel(
        x_ref, h0_ref, cbuf_ref, g_pre_ref, w_in_ref, lbl_ref, cw_ref, cb_ref,
        wa_ref, ba_ref, wx_ref, bx_ref, ll_ref,
        ft_ref, kt_ref, q_ref, v_ref, og_ref, hb_ref, ga_ref, gb_ref, h_out_ref, c_out_ref):
    d = x_ref.shape[-1]
    xn = _rms(x_ref[...], g_pre_ref[...]).astype(BF16)

    def proj(i):
        return _cdot(xn, [w_in_ref.at[:, i * d:(i + 1) * d]])

    lb = _forget_lower_bound(lbl_ref[...])
    qh = _silu(proj(0))
    f = lb + (1.0 - lb) * jax.nn.sigmoid(proj(1))
    kk = 1.0 - f
    for h in range(HEADS):
        sl = slice(h * HEAD_DIM, (h + 1) * HEAD_DIM)
        ft_ref[sl, :] = f[:, sl].T
        kt_ref[sl, :] = kk[:, sl].T
    q_ref[...] = qh.astype(BF16)
    v_ref[...] = proj(2)
    og_ref[...] = _silu(proj(3))

    lx = proj(4)
    cw = cw_ref[...]
    xc = cb_ref[...] + cbuf_ref[0] * cw[0:1]
    xc = xc + cbuf_ref[1] * cw[1:2]
    xc = xc + cbuf_ref[2] * cw[2:3]
    xc = xc + lx * cw[3:4]
    c_out_ref[0] = cbuf_ref[1]
    c_out_ref[1] = cbuf_ref[2]
    c_out_ref[2] = lx
    a, mult, gate_in = _lru_gates(xc, wa_ref, wx_ref, ba_ref[...], bx_ref[...], ll_ref[...])
    hnew = a * h0_ref[...] + mult * gate_in
    h_out_ref[...] = hnew
    hb_ref[...] = (_gelu_tanh(proj(5)) * hnew).astype(BF16)
    ga_ref[...] = jax.nn.sigmoid(proj(6))
    gb_ref[...] = jax.nn.sigmoid(proj(7))


def _sample_state_kernel(s_ref, ft_ref, kt_ref, q_ref, v_ref, s_out_ref, o_ref, *, bb):
    i = pl.program_id(0)
    lanes = ft_ref.shape[-1]
    shift = lax.rem(lanes - i * bb, lanes)
    ft = pltpu.roll(ft_ref[...], shift, 1)
    kt = pltpu.roll(kt_ref[...], shift, 1)
    for u in range(bb):
        for h in range(HEADS):
            sl = slice(h * HEAD_DIM, (h + 1) * HEAD_DIM)
            s_new = ft[sl, u:u + 1] * s_ref[u, h] + kt[sl, u:u + 1] * v_ref[u:u + 1, sl]
            s_out_ref[u, h] = s_new
            o_ref[u:u + 1, sl] = jnp.dot(q_ref[u:u + 1, sl], s_new.astype(BF16),
                                         preferred_element_type=F32)


def _sample_tail_kernel(
        x_ref, o_ref, og_ref, hb_ref, ga_ref, gb_ref, onorm_ref, g_post_ref, w_pa_ref, w_pb_ref,
        w_o_ref, g_fpre_ref, g_fpost_ref, w_fi_ref, w_fo_ref, y_ref):
    o = o_ref[...]
    og = og_ref[...]
    oa = jnp.concatenate(
        [_head_norm_gate(o[:, h * HEAD_DIM:(h + 1) * HEAD_DIM],
                         og[:, h * HEAD_DIM:(h + 1) * HEAD_DIM], onorm_ref[...])
         for h in range(HEADS)], axis=-1).astype(BF16)
    x1 = _merge_out(x_ref[...], oa, hb_ref[...], ga_ref[...], gb_ref[...], g_post_ref[...],
                    [w_pa_ref], [w_pb_ref], [w_o_ref])
    d_ff = w_fo_ref.shape[0]
    w_fi = _col_refs(w_fi_ref, d_ff)
    y_ref[...] = _ffn(x1, g_fpre_ref[...], g_fpost_ref[...], w_fi[:1], w_fi[1:], [w_fo_ref])


def _sample_step(x, s0, h0, cbuf, p, bb=16):
    nb, d = x.shape
    assert nb == HEAD_DIM and nb % bb == 0
    cp = pltpu.CompilerParams(vmem_limit_bytes=V7X_VMEM_LIMIT_BYTES)
    tok = jax.ShapeDtypeStruct((nb, d), F32)
    tok16 = jax.ShapeDtypeStruct((nb, d), BF16)
    col = jax.ShapeDtypeStruct((d, nb), F32)
    ft, kt, q, v, og, hb, ga, gb, h_new, c_new = pl.pallas_call(
        _sample_token_kernel,
        out_shape=(col, col, tok16, tok, tok, tok16, tok, tok, tok,
                   jax.ShapeDtypeStruct((CONV_W - 1, nb, d), F32)),
        name="sample_token", compiler_params=cp,
    )(x, h0, cbuf, p["g_mix_pre"], p["w_in"], p["lb_logits"], p["conv_w"], p["conv_b"],
      p["wa"], p["ba"], p["wx"], p["bx"], p["log_lambda"])

    s_new, o = pl.pallas_call(
        functools.partial(_sample_state_kernel, bb=bb), grid=(nb // bb,),
        in_specs=[pl.BlockSpec((bb, HEADS, HEAD_DIM, HEAD_DIM), lambda i: (i, 0, 0, 0)),
                  _full((d, nb)), _full((d, nb)),
                  pl.BlockSpec((bb, d), lambda i: (i, 0)), pl.BlockSpec((bb, d), lambda i: (i, 0))],
        out_specs=(pl.BlockSpec((bb, HEADS, HEAD_DIM, HEAD_DIM), lambda i: (i, 0, 0, 0)),
                   pl.BlockSpec((bb, d), lambda i: (i, 0))),
        out_shape=(jax.ShapeDtypeStruct(s0.shape, F32), tok),
        name="sample_state",
        compiler_params=pltpu.CompilerParams(
            dimension_semantics=("arbitrary",), vmem_limit_bytes=V7X_VMEM_LIMIT_BYTES),
    )(s0, ft, kt, q, v)

    y = pl.pallas_call(
        _sample_tail_kernel, out_shape=tok, name="sample_tail", compiler_params=cp,
    )(x, o, og, hb, ga, gb, p["onorm"], p["g_mix_post"], p["w_pa"], p["w_pb"], p["w_o"],
      p["g_ffn_pre"], p["g_ffn_post"], p["w_fi"], p["w_fo"])
    return y, s_new, h_new, c_new


def kernel(x_prompt, x_sample, state_hgrn, state_lru, state_conv, norm_mix_pre, norm_mix_post,
           norm_ffn_pre, norm_ffn_post, w_in, hg_lb_logits, hg_out_norm, lru_conv_w, lru_conv_b,
           lru_w_a, lru_b_a, lru_w_x, lru_b_x, lru_log_lambda, w_branch_a, w_branch_b, w_out,
           w_ffn_in, w_ffn_out):
    depth = w_in.shape[0]
    assert depth == 1 and hg_lb_logits.shape[0] == 2
    nb, t, d = x_prompt.shape
    nbs = x_sample.shape[0]
    assert x_sample.shape[1] == 1

    def row(a):
        return a.reshape(1, -1).astype(F32)

    p = dict(
        g_mix_pre=row(norm_mix_pre[0]), g_mix_post=row(norm_mix_post[0]),
        g_ffn_pre=row(norm_ffn_pre[0]), g_ffn_post=row(norm_ffn_post[0]),
        w_in=w_in[0].astype(BF16), lb_logits=hg_lb_logits.astype(F32),
        onorm=row(hg_out_norm[0]), conv_w=lru_conv_w[0].astype(F32), conv_b=row(lru_conv_b[0]),
        wa=lru_w_a[0].astype(BF16), ba=row(lru_b_a[0]), wx=lru_w_x[0].astype(BF16),
        bx=row(lru_b_x[0]), log_lambda=row(lru_log_lambda[0]),
        w_pa=w_branch_a[0].astype(BF16), w_pb=w_branch_b[0].astype(BF16),
        w_o=w_out[0].astype(BF16), w_fi=w_ffn_in[0].astype(BF16), w_fo=w_ffn_out[0].astype(BF16),
    )

    x1, s_p, h_p, c_p = _prompt_mixer(x_prompt, p, tm=512)
    y_p = _ffn_call(x1.reshape(nb * t, d), p, tm=1024).reshape(nb, t, d)

    y_s, s_s, h_s, c_s = _sample_step(
        x_sample.reshape(nbs, d), state_hgrn[0], state_lru[0],
        jnp.swapaxes(state_conv[0], 0, 1), p)

    return (y_p, y_s.reshape(nbs, 1, d), s_p[None], h_p.reshape(1, nb, d), c_p[None],
            s_s[None], h_s[None], jnp.swapaxes(c_s, 0, 1)[None])
```

```python
import functools

import jax
import jax.numpy as jnp
from jax import lax
from jax.experimental import pallas as pl
from jax.experimental.pallas import tpu as pltpu

F32 = jnp.float32
BF16 = jnp.bfloat16

HEADS = 8
HEAD_DIM = 128
LRU_BLOCKS = 8
LRU_BLOCK_W = 128
CONV_W = 4
LRU_C = 8.0
EPS = 1e-6
CHUNK = 64
SUB = 16
SLAB = 8
PROJ_COLS = 512
FFN_COLS = 256
V7X_VMEM_LIMIT_BYTES = 56 * 1024 * 1024


def _rms(x, g):
    ms = jnp.mean(x * x, axis=-1, keepdims=True)
    return x * lax.rsqrt(ms + EPS) * g


def _dot_nt(a, b):
    return lax.dot_general(a, b, (((1,), (1,)), ((), ())), preferred_element_type=F32)


def _dot_tn(a, b):
    return lax.dot_general(a, b, (((0,), (0,)), ((), ())), preferred_element_type=F32)


def _silu(x):
    return x * jax.nn.sigmoid(x)


def _gelu_tanh(x):
    c = 0.7978845608028654
    return x * (0.5 * (1.0 + jnp.tanh(c * (x + 0.044715 * (x * x * x)))))


def _softplus(z):
    return jnp.maximum(z, 0.0) + jnp.log1p(jnp.exp(-jnp.abs(z)))


def _forget_lower_bound(logits):
    m = jnp.max(logits, axis=0, keepdims=True)
    e = jnp.exp(logits - m)
    return e[0:1] / (e[0:1] + e[1:2])


def _lru_gates(xc, wa_ref, wx_ref, ba, bx, log_lambda):
    ra, ix = [], []
    for n in range(LRU_BLOCKS):
        sl = slice(n * LRU_BLOCK_W, (n + 1) * LRU_BLOCK_W)
        xb = xc[:, sl].astype(BF16)
        ra.append(jnp.dot(xb, wa_ref[n], preferred_element_type=F32))
        ix.append(jnp.dot(xb, wx_ref[n], preferred_element_type=F32))
    ra = jnp.concatenate(ra, axis=-1) + ba
    ix = jnp.concatenate(ix, axis=-1) + bx
    log_a = (-LRU_C) * jax.nn.sigmoid(ra) * _softplus(-log_lambda)
    a = jnp.exp(log_a)
    mult = jnp.sqrt(-jnp.tanh(log_a) * (a * a + 1.0))
    gate_in = jax.nn.sigmoid(ix) * xc
    return a, mult, gate_in


def _head_norm_gate(oh, og_silu, onorm):
    ms = jnp.mean(oh * oh, axis=-1, keepdims=True)
    return oh * lax.rsqrt(ms + EPS) * onorm * og_silu


def _col_refs(w_ref, cols):
    return [w_ref.at[:, c * cols:(c + 1) * cols] for c in range(w_ref.shape[1] // cols)]


def _col_specs(w, cols):
    n = w.shape[1] // cols
    assert n * cols == w.shape[1]
    specs = [pl.BlockSpec((w.shape[0], cols), functools.partial(lambda c, *_: (0, c), c),
                          pipeline_mode=pl.Buffered(1)) for c in range(n)]
    return specs, [w] * n


def _cdot(a, w_chunks):
    parts = [jnp.dot(a, w[...], preferred_element_type=F32) for w in w_chunks]
    return parts[0] if len(parts) == 1 else jnp.concatenate(parts, axis=-1)


def _merge_out(x, oa, hb, ga_sig, gb_sig, g_post, w_pa, w_pb, w_o):
    merged = ga_sig * _cdot(oa, w_pa) + gb_sig * _cdot(hb, w_pb)
    return x + _rms(_cdot(merged.astype(BF16), w_o), g_post)


def _ffn(x, g_pre, g_post, w_gate, w_up, w_fo):
    hn = _rms(x, g_pre).astype(BF16)
    act = jnp.concatenate(
        [(_silu(_cdot(hn, [wg])) * _cdot(hn, [wu])).astype(BF16)
         for wg, wu in zip(w_gate, w_up, strict=True)], axis=-1)
    return x + _rms(_cdot(act, w_fo), g_post)


def _chunk_prefix_sum(b2_ref, r0, sl, row8):
    slabs, carry = [], None
    for i in range(CHUNK // SLAB):
        vv = b2_ref[r0 + i * SLAB:r0 + (i + 1) * SLAB, sl]
        for s in (1, 2, 4):
            vv = vv + jnp.where(row8 >= s, pltpu.roll(vv, s, 0), 0.0)
        if carry is not None:
            vv = vv + carry
        carry = vv[SLAB - 1:SLAB]
        slabs.append(vv)
    return jnp.concatenate(slabs, axis=0)


def _prompt_mixer_kernel(*refs, tm):
    (x_ref, g_pre_ref, g_post_ref, lbl_ref, onorm_ref, cw_ref, cb_ref,
     wa_ref, ba_ref, wx_ref, bx_ref, ll_ref) = refs[:12]
    d = x_ref.shape[-1]
    per = d // PROJ_COLS
    weights = refs[12:12 + 11 * per]
    w_in, w_pa, w_pb, w_o = (weights[:8 * per], weights[8 * per:9 * per],
                             weights[9 * per:10 * per], weights[10 * per:])
    (y_ref, s_out_ref, h_out_ref, c_out_ref,
     st_ref, h_ref, xp_ref, xn_s, qh_s, b2_s, kk_s, v_s, og_s, ly_s, ga_s, gb_s, hg_s, hs_s,
     qin_s, a_s, u_s, dec_s, la_s, lb_s) = refs[12 + 11 * per:]
    j = pl.program_id(1)
    nchunk = tm // CHUNK
    nsub = CHUNK // SUB

    @pl.when(j == 0)
    def _():
        st_ref[...] = jnp.zeros_like(st_ref)
        h_ref[...] = jnp.zeros_like(h_ref)
        xp_ref[0:SLAB, :] = jnp.zeros((SLAB, d), F32)

    def proj(g, c0):
        return jnp.dot(xn_s[...], w_in[(g * d + c0) // PROJ_COLS][...],
                       preferred_element_type=F32)

    def project_recurrence_inputs():
        xn_s[...] = _rms(x_ref[0], g_pre_ref[...]).astype(BF16)
        for c0 in range(0, d, PROJ_COLS):
            xp_ref[SLAB:SLAB + tm, c0:c0 + PROJ_COLS] = proj(4, c0)
        lb = _forget_lower_bound(lbl_ref[...])
        for c0 in range(0, d, PROJ_COLS):
            cs = slice(c0, c0 + PROJ_COLS)
            qh_s[:, cs] = _silu(proj(0, c0))
            f = lb[:, cs] + (1.0 - lb[:, cs]) * jax.nn.sigmoid(proj(1, c0))
            b2_s[:, cs] = jnp.log2(f)
            kk_s[:, cs] = 1.0 - f
            v_s[:, cs] = proj(2, c0).astype(BF16)

    def rglru_gates():
        cw = cw_ref[...]
        xc = cb_ref[...]
        for k in range(CONV_W):
            r0 = SLAB - (CONV_W - 1) + k
            xc = xc + xp_ref[r0:r0 + tm, :] * cw[k:k + 1]
        xp_ref[0:SLAB, :] = xp_ref[tm:tm + SLAB, :]
        a, mult, gate_in = _lru_gates(xc, wa_ref, wx_ref, ba_ref[...], bx_ref[...], ll_ref[...])
        row = lax.broadcasted_iota(jnp.int32, (tm, d), 0)
        mult = jnp.where(jnp.logical_and(row == 0, j == 0), 1.0, mult)
        la_s[...] = a
        lb_s[...] = mult * gate_in

    def project_gates():
        for c0 in range(0, d, PROJ_COLS):
            cs = slice(c0, c0 + PROJ_COLS)
            og_s[:, cs] = _silu(proj(3, c0))
            ly_s[:, cs] = _gelu_tanh(proj(5, c0))
            ga_s[:, cs] = jax.nn.sigmoid(proj(6, c0))
            gb_s[:, cs] = jax.nn.sigmoid(proj(7, c0))

    def hgrn2_scores():
        row8 = lax.broadcasted_iota(jnp.int32, (SLAB, HEAD_DIM), 0)
        for h in range(HEADS):
            sl = slice(h * HEAD_DIM, (h + 1) * HEAD_DIM)
            for c in range(nchunk):
                r0 = c * CHUNK
                rs = slice(r0, r0 + CHUNK)
                bc = _chunk_prefix_sum(b2_s, r0, sl, row8)
                qc, kc, vc = qh_s[rs, sl], kk_s[rs, sl], v_s[rs, sl]
                b_last = bc[CHUNK - 1:CHUNK]
                qin_s[rs, sl] = (qc * jnp.exp2(bc)).astype(BF16)
                dec_s[c:c + 1, sl] = jnp.exp2(b_last)
                for ib in range(nsub):
                    t0, t1 = ib * SUB, (ib + 1) * SUB
                    ref_b = bc[t0 + SUB // 2 - 1:t0 + SUB // 2]
                    qs = (qc[t0:t1] * jnp.exp2(bc[t0:t1] - ref_b)).astype(BF16)
                    ks = (kc[:t1] * jnp.exp2(ref_b - bc[:t1])).astype(BF16)
                    a_blk = _dot_nt(qs, ks)
                    ti = lax.broadcasted_iota(jnp.int32, a_blk.shape, 0) + t0
                    si = lax.broadcasted_iota(jnp.int32, a_blk.shape, 1)
                    a_s[r0 + t0:r0 + t1, h * HEAD_DIM:h * HEAD_DIM + t1] = (
                        jnp.where(ti >= si, a_blk, 0.0).astype(BF16))
                kst = (kc * jnp.exp2(b_last - bc)).astype(BF16)
                u_s[h * nchunk + c] = _dot_tn(vc, kst)

    def hgrn2_recurrence():
        for h in range(HEADS):
            sl = slice(h * HEAD_DIM, (h + 1) * HEAD_DIM)
            st = st_ref[h]
            for c in range(nchunk):
                r0 = c * CHUNK
                rs = slice(r0, r0 + CHUNK)
                o_c = _dot_nt(qin_s[rs, sl], st.astype(BF16))
                rows = []
                for ib in range(nsub):
                    t0, t1 = ib * SUB, (ib + 1) * SUB
                    a_blk = a_s[r0 + t0:r0 + t1, h * HEAD_DIM:h * HEAD_DIM + t1]
                    rows.append(o_c[t0:t1] + jnp.dot(a_blk, v_s[r0:r0 + t1, sl],
                                                     preferred_element_type=F32))
                hg_s[rs, sl] = jnp.concatenate(rows, axis=0)
                st = dec_s[c:c + 1, sl] * st + u_s[h * nchunk + c]
            st_ref[h] = st

    def rglru_scan():
        row8d = lax.broadcasted_iota(jnp.int32, (SLAB, d), 0)
        hc = h_ref[...]
        for i in range(tm // SLAB):
            a8 = la_s[i * SLAB:(i + 1) * SLAB, :]
            b8 = lb_s[i * SLAB:(i + 1) * SLAB, :]
            for s in (1, 2, 4):
                m = row8d >= s
                b8 = b8 + a8 * jnp.where(m, pltpu.roll(b8, s, 0), 0.0)
                a8 = a8 * jnp.where(m, pltpu.roll(a8, s, 0), 1.0)
            h8 = a8 * hc + b8
            hc = h8[SLAB - 1:SLAB]
            hs_s[i * SLAB:(i + 1) * SLAB, :] = h8
        h_ref[...] = hc

    def merge_out():
        onorm = onorm_ref[...]
        oa = jnp.concatenate(
            [_head_norm_gate(hg_s[:, h * HEAD_DIM:(h + 1) * HEAD_DIM],
                             og_s[:, h * HEAD_DIM:(h + 1) * HEAD_DIM], onorm)
             for h in range(HEADS)], axis=-1).astype(BF16)
        hb = (ly_s[...] * hs_s[...]).astype(BF16)
        y_ref[0] = _merge_out(x_ref[0], oa, hb, ga_s[...], gb_s[...], g_post_ref[...],
                              w_pa, w_pb, w_o)

    project_recurrence_inputs()
    rglru_gates()
    project_gates()
    hgrn2_scores()
    hgrn2_recurrence()
    rglru_scan()
    merge_out()

    @pl.when(j == pl.num_programs(1) - 1)
    def _():
        for h in range(HEADS):
            s_out_ref[0, h] = st_ref[h].T
        h_out_ref[0] = h_ref[...]
        c_out_ref[0] = xp_ref[SLAB - (CONV_W - 1):SLAB, :]


def _full(shape):
    return pl.BlockSpec(shape, lambda *_: (0,) * len(shape))


def _prompt_mixer(x, p, tm):
    nb, t, d = x.shape
    assert t % tm == 0 and tm % CHUNK == 0 and d == HEADS * HEAD_DIM and d % PROJ_COLS == 0
    assert tm // CHUNK <= SLAB and CONV_W - 1 <= SLAB
    kern = functools.partial(_prompt_mixer_kernel, tm=tm)
    in_specs = [
        pl.BlockSpec((1, tm, d), lambda s, j: (s, j, 0)),
        _full((1, d)), _full((1, d)), _full(p["lb_logits"].shape),
        _full((1, HEAD_DIM)), _full((CONV_W, d)), _full((1, d)),
        _full(p["wa"].shape), _full((1, d)), _full(p["wx"].shape), _full((1, d)), _full((1, d)),
    ]
    weights = []
    for name in ("w_in", "w_pa", "w_pb", "w_o"):
        specs, ops = _col_specs(p[name], PROJ_COLS)
        in_specs += specs
        weights += ops
    out_shape = (
        jax.ShapeDtypeStruct((nb, t, d), F32),
        jax.ShapeDtypeStruct((nb, HEADS, HEAD_DIM, HEAD_DIM), F32),
        jax.ShapeDtypeStruct((nb, 1, d), F32),
        jax.ShapeDtypeStruct((nb, CONV_W - 1, d), F32),
    )
    out_specs = (
        pl.BlockSpec((1, tm, d), lambda s, j: (s, j, 0)),
        pl.BlockSpec((1, HEADS, HEAD_DIM, HEAD_DIM), lambda s, j: (s, 0, 0, 0)),
        pl.BlockSpec((1, 1, d), lambda s, j: (s, 0, 0)),
        pl.BlockSpec((1, CONV_W - 1, d), lambda s, j: (s, 0, 0)),
    )
    tile_f32 = pltpu.VMEM((tm, d), F32)
    tile_bf16 = pltpu.VMEM((tm, d), BF16)
    scratch = [
        pltpu.VMEM((HEADS, HEAD_DIM, HEAD_DIM), F32),
        pltpu.VMEM((1, d), F32),
        pltpu.VMEM((tm + SLAB, d), F32),
        tile_bf16,
        tile_f32, tile_f32, tile_f32,
        tile_bf16,
        tile_f32, tile_f32, tile_f32, tile_f32,
        tile_f32,
        tile_f32,
        tile_bf16,
        tile_bf16,
        pltpu.VMEM((HEADS * (tm // CHUNK), HEAD_DIM, HEAD_DIM), F32),
        pltpu.VMEM((SLAB, d), F32),
        tile_f32, tile_f32,
    ]
    return pl.pallas_call(
        kern, grid=(nb, t // tm), in_specs=in_specs, out_specs=out_specs, out_shape=out_shape,
        scratch_shapes=scratch, name="prompt_mixer",
        compiler_params=pltpu.CompilerParams(
            dimension_semantics=("arbitrary", "arbitrary"),
            vmem_limit_bytes=V7X_VMEM_LIMIT_BYTES),
    )(x, p["g_mix_pre"], p["g_mix_post"], p["lb_logits"], p["onorm"], p["conv_w"],
      p["conv_b"], p["wa"], p["ba"], p["wx"], p["bx"], p["log_lambda"], *weights)


def _ffn_kernel(*refs, nf):
    x_ref, g_pre_ref, g_post_ref = refs[:3]
    w_gate, w_up, w_fo = refs[3:3 + nf], refs[3 + nf:3 + 2 * nf], refs[3 + 2 * nf:-1]
    refs[-1][...] = _ffn(x_ref[...], g_pre_ref[...], g_post_ref[...], w_gate, w_up, w_fo)


def _ffn_call(x, p, tm):
    n, d = x.shape
    assert n % tm == 0
    fi_specs, fi_ops = _col_specs(p["w_fi"], FFN_COLS)
    fo_specs, fo_ops = _col_specs(p["w_fo"], PROJ_COLS)
    return pl.pallas_call(
        functools.partial(_ffn_kernel, nf=len(fi_ops) // 2), grid=(n // tm,),
        in_specs=[pl.BlockSpec((tm, d), lambda i: (i, 0)), _full((1, d)), _full((1, d))]
        + fi_specs + fo_specs,
        out_specs=pl.BlockSpec((tm, d), lambda i: (i, 0)),
        out_shape=jax.ShapeDtypeStruct((n, d), F32), name="swiglu",
        compiler_params=pltpu.CompilerParams(
            dimension_semantics=("arbitrary",), vmem_limit_bytes=V7X_VMEM_LIMIT_BYTES),
    )(x, p["g_ffn_pre"], p["g_ffn_post"], *fi_ops, *fo_ops)


def _sample_token_kernel(
        x_ref, h0_ref, cbuf_ref, g_pre_ref, w_in_ref, lbl_ref, cw_ref, cb_ref,
        wa_ref, ba_ref, wx_ref, bx_ref, ll_ref,
        ft_ref, kt_ref, q_ref, v_ref, og_ref, hb_ref, ga_ref, gb_ref, h_out_ref, c_out_ref):
    d = x_ref.shape[-1]
    xn = _rms(x_ref[...], g_pre_ref[...]).astype(BF16)

    def proj(i):
        return _cdot(xn, [w_in_ref.at[:, i * d:(i + 1) * d]])

    lb = _forget_lower_bound(lbl_ref[...])
    qh = _silu(proj(0))
    f = lb + (1.0 - lb) * jax.nn.sigmoid(proj(1))
    kk = 1.0 - f
    for h in range(HEADS):
        sl = slice(h * HEAD_DIM, (h + 1) * HEAD_DIM)
        ft_ref[sl, :] = f[:, sl].T
        kt_ref[sl, :] = kk[:, sl].T
    q_ref[...] = qh.astype(BF16)
    v_ref[...] = proj(2)
    og_ref[...] = _silu(proj(3))

    lx = proj(4)
    cw = cw_ref[...]
    taps = [cbuf_ref[k] for k in range(CONV_W - 1)] + [lx]
    xc = cb_ref[...]
    for k in range(CONV_W):
        xc = xc + taps[k] * cw[k:k + 1]
    for k in range(CONV_W - 1):
        c_out_ref[k] = taps[k + 1]
    a, mult, gate_in = _lru_gates(xc, wa_ref, wx_ref, ba_ref[...], bx_ref[...], ll_ref[...])
    hnew = a * h0_ref[...] + mult * gate_in
    h_out_ref[...] = hnew
    hb_ref[...] = (_gelu_tanh(proj(5)) * hnew).astype(BF16)
    ga_ref[...] = jax.nn.sigmoid(proj(6))
    gb_ref[...] = jax.nn.sigmoid(proj(7))


def _sample_state_kernel(s_ref, ft_ref, kt_ref, q_ref, v_ref, s_out_ref, o_ref, *, bb):
    i = pl.program_id(0)
    lanes = ft_ref.shape[-1]
    shift = lax.rem(lanes - i * bb, lanes)
    ft = pltpu.roll(ft_ref[...], shift, 1)
    kt = pltpu.roll(kt_ref[...], shift, 1)
    for u in range(bb):
        for h in range(HEADS):
            sl = slice(h * HEAD_DIM, (h + 1) * HEAD_DIM)
            s_new = ft[sl, u:u + 1] * s_ref[u, h] + kt[sl, u:u + 1] * v_ref[u:u + 1, sl]
            s_out_ref[u, h] = s_new
            o_ref[u:u + 1, sl] = jnp.dot(q_ref[u:u + 1, sl], s_new.astype(BF16),
                                         preferred_element_type=F32)


def _sample_tail_kernel(
        x_ref, o_ref, og_ref, hb_ref, ga_ref, gb_ref, onorm_ref, g_post_ref, w_pa_ref, w_pb_ref,
        w_o_ref, g_fpre_ref, g_fpost_ref, w_fi_ref, w_fo_ref, y_ref):
    o = o_ref[...]
    og = og_ref[...]
    oa = jnp.concatenate(
        [_head_norm_gate(o[:, h * HEAD_DIM:(h + 1) * HEAD_DIM],
                         og[:, h * HEAD_DIM:(h + 1) * HEAD_DIM], onorm_ref[...])
         for h in range(HEADS)], axis=-1).astype(BF16)
    x1 = _merge_out(x_ref[...], oa, hb_ref[...], ga_ref[...], gb_ref[...], g_post_ref[...],
                    [w_pa_ref], [w_pb_ref], [w_o_ref])
    d_ff = w_fo_ref.shape[0]
    w_fi = _col_refs(w_fi_ref, d_ff)
    y_ref[...] = _ffn(x1, g_fpre_ref[...], g_fpost_ref[...], w_fi[:1], w_fi[1:], [w_fo_ref])


def _sample_step(x, s0, h0, cbuf, p, bb=16):
    nb, d = x.shape
    assert nb == HEAD_DIM and nb % bb == 0
    cp = pltpu.CompilerParams(vmem_limit_bytes=V7X_VMEM_LIMIT_BYTES)
    tok = jax.ShapeDtypeStruct((nb, d), F32)
    tok16 = jax.ShapeDtypeStruct((nb, d), BF16)
    col = jax.ShapeDtypeStruct((d, nb), F32)
    ft, kt, q, v, og, hb, ga, gb, h_new, c_new = pl.pallas_call(
        _sample_token_kernel,
        out_shape=(col, col, tok16, tok, tok, tok16, tok, tok, tok,
                   jax.ShapeDtypeStruct((CONV_W - 1, nb, d), F32)),
        name="sample_token", compiler_params=cp,
    )(x, h0, cbuf, p["g_mix_pre"], p["w_in"], p["lb_logits"], p["conv_w"], p["conv_b"],
      p["wa"], p["ba"], p["wx"], p["bx"], p["log_lambda"])

    s_new, o = pl.pallas_call(
        functools.partial(_sample_state_kernel, bb=bb), grid=(nb // bb,),
        in_specs=[pl.BlockSpec((bb, HEADS, HEAD_DIM, HEAD_DIM), lambda i: (i, 0, 0, 0)),
                  _full((d, nb)), _full((d, nb)),
                  pl.BlockSpec((bb, d), lambda i: (i, 0)), pl.BlockSpec((bb, d), lambda i: (i, 0))],
        out_specs=(pl.BlockSpec((bb, HEADS, HEAD_DIM, HEAD_DIM), lambda i: (i, 0, 0, 0)),
                   pl.BlockSpec((bb, d), lambda i: (i, 0))),
        out_shape=(jax.ShapeDtypeStruct(s0.shape, F32), tok),
        name="sample_state",
        compiler_params=pltpu.CompilerParams(
            dimension_semantics=("arbitrary",), vmem_limit_bytes=V7X_VMEM_LIMIT_BYTES),
    )(s0, ft, kt, q, v)

    y = pl.pallas_call(
        _sample_tail_kernel, out_shape=tok, name="sample_tail", compiler_params=cp,
    )(x, o, og, hb, ga, gb, p["onorm"], p["g_mix_post"], p["w_pa"], p["w_pb"], p["w_o"],
      p["g_ffn_pre"], p["g_ffn_post"], p["w_fi"], p["w_fo"])
    return y, s_new, h_new, c_new


def kernel(x_prompt, x_sample, state_hgrn, state_lru, state_conv, norm_mix_pre, norm_mix_post,
           norm_ffn_pre, norm_ffn_post, w_in, hg_lb_logits, hg_out_norm, lru_conv_w, lru_conv_b,
           lru_w_a, lru_b_a, lru_w_x, lru_b_x, lru_log_lambda, w_branch_a, w_branch_b, w_out,
           w_ffn_in, w_ffn_out):
    depth = w_in.shape[0]
    assert depth == 1 and hg_lb_logits.shape[0] == 2
    nb, t, d = x_prompt.shape
    nbs = x_sample.shape[0]
    assert x_sample.shape[1] == 1

    def row(a):
        return a.reshape(1, -1).astype(F32)

    p = dict(
        g_mix_pre=row(norm_mix_pre[0]), g_mix_post=row(norm_mix_post[0]),
        g_ffn_pre=row(norm_ffn_pre[0]), g_ffn_post=row(norm_ffn_post[0]),
        w_in=w_in[0].astype(BF16), lb_logits=hg_lb_logits.astype(F32),
        onorm=row(hg_out_norm[0]), conv_w=lru_conv_w[0].astype(F32), conv_b=row(lru_conv_b[0]),
        wa=lru_w_a[0].astype(BF16), ba=row(lru_b_a[0]), wx=lru_w_x[0].astype(BF16),
        bx=row(lru_b_x[0]), log_lambda=row(lru_log_lambda[0]),
        w_pa=w_branch_a[0].astype(BF16), w_pb=w_branch_b[0].astype(BF16),
        w_o=w_out[0].astype(BF16), w_fi=w_ffn_in[0].astype(BF16), w_fo=w_ffn_out[0].astype(BF16),
    )

    x1, s_p, h_p, c_p = _prompt_mixer(x_prompt, p, tm=256)
    y_p = _ffn_call(x1.reshape(nb * t, d), p, tm=1024).reshape(nb, t, d)

    y_s, s_s, h_s, c_s = _sample_step(
        x_sample.reshape(nbs, d), state_hgrn[0], state_lru[0],
        jnp.swapaxes(state_conv[0], 0, 1), p)

    return (y_p, y_s.reshape(nbs, 1, d), s_p[None], h_p.reshape(1, nb, d), c_p[None],
            s_s[None], h_s[None], jnp.swapaxes(c_s, 0, 1)[None])
```

```python
import functools

import jax
import jax.numpy as jnp
from jax import lax
from jax.experimental import pallas as pl
from jax.experimental.pallas import tpu as pltpu

F32 = jnp.float32
BF16 = jnp.bfloat16

HEADS = 8
HEAD_DIM = 128
LRU_BLOCKS = 8
LRU_BLOCK_W = 128
CONV_W = 4
LRU_C = 8.0
EPS = 1e-6
CHUNK = 64
SUB = 16
SLAB = 8
PROJ_COLS = 512
FFN_COLS = 256
V7X_VMEM_LIMIT_BYTES = 56 * 1024 * 1024


def _rms(x, g):
    ms = jnp.mean(x * x, axis=-1, keepdims=True)
    return x * lax.rsqrt(ms + EPS) * g


def _dot_nt(a, b):
    return lax.dot_general(a, b, (((1,), (1,)), ((), ())), preferred_element_type=F32)


def _dot_tn(a, b):
    return lax.dot_general(a, b, (((0,), (0,)), ((), ())), preferred_element_type=F32)


def _silu(x):
    return x * jax.nn.sigmoid(x)


def _gelu_tanh(x):
    c = 0.7978845608028654
    return x * (0.5 * (1.0 + jnp.tanh(c * (x + 0.044715 * (x * x * x)))))


def _softplus(z):
    return jnp.maximum(z, 0.0) + jnp.log1p(jnp.exp(-jnp.abs(z)))


def _forget_lower_bound(logits):
    m = jnp.max(logits, axis=0, keepdims=True)
    e = jnp.exp(logits - m)
    return e[0:1] / (e[0:1] + e[1:2])


def _lru_gates(xc, wa_ref, wx_ref, ba, bx, log_lambda):
    ra, ix = [], []
    for n in range(LRU_BLOCKS):
        sl = slice(n * LRU_BLOCK_W, (n + 1) * LRU_BLOCK_W)
        xb = xc[:, sl].astype(BF16)
        ra.append(jnp.dot(xb, wa_ref[n], preferred_element_type=F32))
        ix.append(jnp.dot(xb, wx_ref[n], preferred_element_type=F32))
    ra = jnp.concatenate(ra, axis=-1) + ba
    ix = jnp.concatenate(ix, axis=-1) + bx
    log_a = (-LRU_C) * jax.nn.sigmoid(ra) * _softplus(-log_lambda)
    a = jnp.exp(log_a)
    mult = jnp.sqrt(-jnp.tanh(log_a) * (a * a + 1.0))
    gate_in = jax.nn.sigmoid(ix) * xc
    return a, mult, gate_in


def _head_norm_gate(oh, og_silu, onorm):
    ms = jnp.mean(oh * oh, axis=-1, keepdims=True)
    return oh * lax.rsqrt(ms + EPS) * onorm * og_silu


def _col_refs(w_ref, cols):
    return [w_ref.at[:, c * cols:(c + 1) * cols] for c in range(w_ref.shape[1] // cols)]


def _col_specs(w, cols):
    n = w.shape[1] // cols
    assert n * cols == w.shape[1]
    specs = [pl.BlockSpec((w.shape[0], cols), functools.partial(lambda c, *_: (0, c), c),
                          pipeline_mode=pl.Buffered(1)) for c in range(n)]
    return specs, [w] * n


def _cdot(a, w_chunks):
    parts = [jnp.dot(a, w[...], preferred_element_type=F32) for w in w_chunks]
    return parts[0] if len(parts) == 1 else jnp.concatenate(parts, axis=-1)


def _merge_out(x, oa, hb, ga_sig, gb_sig, g_post, w_pa, w_pb, w_o):
    merged = ga_sig * _cdot(oa, w_pa) + gb_sig * _cdot(hb, w_pb)
    return x + _rms(_cdot(merged.astype(BF16), w_o), g_post)


def _ffn(x, g_pre, g_post, w_gate, w_up, w_fo):
    hn = _rms(x, g_pre).astype(BF16)
    act = jnp.concatenate(
        [(_silu(_cdot(hn, [wg])) * _cdot(hn, [wu])).astype(BF16)
         for wg, wu in zip(w_gate, w_up, strict=True)], axis=-1)
    return x + _rms(_cdot(act, w_fo), g_post)


def _chunk_prefix_sum(b2_ref, r0, sl, row8):
    slabs, carry = [], None
    for i in range(CHUNK // SLAB):
        vv = b2_ref[r0 + i * SLAB:r0 + (i + 1) * SLAB, sl]
        for s in (1, 2, 4):
            vv = vv + jnp.where(row8 >= s, pltpu.roll(vv, s, 0), 0.0)
        if carry is not None:
            vv = vv + carry
        carry = vv[SLAB - 1:SLAB]
        slabs.append(vv)
    return jnp.concatenate(slabs, axis=0)


def _sample_state_update(first, count, s_ref, ft_ref, q_ref, v_ref, s_out_ref, o_ref):
    lanes = ft_ref.shape[-1]
    shift = lax.rem(lanes - first, lanes)
    ft = pltpu.roll(ft_ref[...], shift, 1)
    for u in range(count):
        for h in range(HEADS):
            sl = slice(h * HEAD_DIM, (h + 1) * HEAD_DIM)
            v_row = v_ref[0, u:u + 1, sl]
            s_new = ft[sl, u:u + 1] * (s_ref[u, h] - v_row) + v_row
            s_out_ref[u, h] = s_new
            o_ref[0, u:u + 1, sl] = jnp.dot(q_ref[0, u:u + 1, sl].astype(BF16),
                                            s_new.astype(BF16), preferred_element_type=F32)


def _prompt_mixer_kernel(*refs, tm, spb):
    (x_ref, g_pre_ref, g_post_ref, lbl_ref, onorm_ref, cw_ref, cb_ref,
     wa_ref, ba_ref, wx_ref, bx_ref, ll_ref, ss_ref, sft_ref, sq_ref, sv_ref) = refs[:16]
    d = x_ref.shape[-1]
    per = d // PROJ_COLS
    weights = refs[16:16 + 11 * per]
    w_in, w_pa, w_pb, w_o = (weights[:8 * per], weights[8 * per:9 * per],
                             weights[9 * per:10 * per], weights[10 * per:])
    (y_ref, s_out_ref, h_out_ref, c_out_ref, ss_out_ref, so_ref,
     st_ref, h_ref, xp_ref, xn_s, qh_s, b2_s, kk_s, v_s, og_s, ly_s, ga_s, gb_s, hg_s, hs_s,
     qin_s, a_s, u_s, dec_s, la_s, lb_s) = refs[16 + 11 * per:]
    j = pl.program_id(1)
    step = pl.program_id(0) * pl.num_programs(1) + j
    nchunk = tm // CHUNK
    nsub = CHUNK // SUB

    @pl.when(j == 0)
    def _():
        st_ref[...] = jnp.zeros_like(st_ref)
        h_ref[...] = jnp.zeros_like(h_ref)
        xp_ref[0:SLAB, :] = jnp.zeros((SLAB, d), F32)

    def proj(g, c0):
        return jnp.dot(xn_s[...], w_in[(g * d + c0) // PROJ_COLS][...],
                       preferred_element_type=F32)

    def project_recurrence_inputs():
        xn_s[...] = _rms(x_ref[0], g_pre_ref[...]).astype(BF16)
        for c0 in range(0, d, PROJ_COLS):
            xp_ref[SLAB:SLAB + tm, c0:c0 + PROJ_COLS] = proj(4, c0)
        lb = _forget_lower_bound(lbl_ref[...])
        for c0 in range(0, d, PROJ_COLS):
            cs = slice(c0, c0 + PROJ_COLS)
            qh_s[:, cs] = _silu(proj(0, c0))
            f = lb[:, cs] + (1.0 - lb[:, cs]) * jax.nn.sigmoid(proj(1, c0))
            b2_s[:, cs] = jnp.log2(f)
            kk_s[:, cs] = 1.0 - f
            v_s[:, cs] = proj(2, c0).astype(BF16)

    def rglru_gates():
        cw = cw_ref[...]
        xc = cb_ref[...]
        for k in range(CONV_W):
            r0 = SLAB - (CONV_W - 1) + k
            xc = xc + xp_ref[r0:r0 + tm, :] * cw[k:k + 1]
        xp_ref[0:SLAB, :] = xp_ref[tm:tm + SLAB, :]
        a, mult, gate_in = _lru_gates(xc, wa_ref, wx_ref, ba_ref[...], bx_ref[...], ll_ref[...])
        row = lax.broadcasted_iota(jnp.int32, (tm, d), 0)
        mult = jnp.where(jnp.logical_and(row == 0, j == 0), 1.0, mult)
        la_s[...] = a
        lb_s[...] = mult * gate_in

    def project_gates():
        for c0 in range(0, d, PROJ_COLS):
            cs = slice(c0, c0 + PROJ_COLS)
            og_s[:, cs] = _silu(proj(3, c0))
            ly_s[:, cs] = _gelu_tanh(proj(5, c0))
            ga_s[:, cs] = jax.nn.sigmoid(proj(6, c0))
            gb_s[:, cs] = jax.nn.sigmoid(proj(7, c0))

    def hgrn2_scores():
        row8 = lax.broadcasted_iota(jnp.int32, (SLAB, HEAD_DIM), 0)
        for h in range(HEADS):
            sl = slice(h * HEAD_DIM, (h + 1) * HEAD_DIM)
            for c in range(nchunk):
                r0 = c * CHUNK
                rs = slice(r0, r0 + CHUNK)
                bc = _chunk_prefix_sum(b2_s, r0, sl, row8)
                qc, kc, vc = qh_s[rs, sl], kk_s[rs, sl], v_s[rs, sl]
                b_last = bc[CHUNK - 1:CHUNK]
                qin_s[rs, sl] = (qc * jnp.exp2(bc)).astype(BF16)
                dec_s[c:c + 1, sl] = jnp.exp2(b_last)
                for ib in range(nsub):
                    t0, t1 = ib * SUB, (ib + 1) * SUB
                    ref_b = bc[t0 + SUB // 2 - 1:t0 + SUB // 2]
                    qs = (qc[t0:t1] * jnp.exp2(bc[t0:t1] - ref_b)).astype(BF16)
                    ks = (kc[:t1] * jnp.exp2(ref_b - bc[:t1])).astype(BF16)
                    a_blk = _dot_nt(qs, ks)
                    ti = lax.broadcasted_iota(jnp.int32, a_blk.shape, 0) + t0
                    si = lax.broadcasted_iota(jnp.int32, a_blk.shape, 1)
                    a_s[r0 + t0:r0 + t1, h * HEAD_DIM:h * HEAD_DIM + t1] = (
                        jnp.where(ti >= si, a_blk, 0.0).astype(BF16))
                kst = (kc * jnp.exp2(b_last - bc)).astype(BF16)
                u_s[h * nchunk + c] = _dot_tn(vc, kst)

    def hgrn2_recurrence():
        for h in range(HEADS):
            sl = slice(h * HEAD_DIM, (h + 1) * HEAD_DIM)
            st = st_ref[h]
            for c in range(nchunk):
                r0 = c * CHUNK
                rs = slice(r0, r0 + CHUNK)
                o_c = _dot_nt(qin_s[rs, sl], st.astype(BF16))
                rows = []
                for ib in range(nsub):
                    t0, t1 = ib * SUB, (ib + 1) * SUB
                    a_blk = a_s[r0 + t0:r0 + t1, h * HEAD_DIM:h * HEAD_DIM + t1]
                    rows.append(o_c[t0:t1] + jnp.dot(a_blk, v_s[r0:r0 + t1, sl],
                                                     preferred_element_type=F32))
                hg_s[rs, sl] = jnp.concatenate(rows, axis=0)
                st = dec_s[c:c + 1, sl] * st + u_s[h * nchunk + c]
            st_ref[h] = st

    def rglru_scan():
        row8d = lax.broadcasted_iota(jnp.int32, (SLAB, d), 0)
        hc = h_ref[...]
        for i in range(tm // SLAB):
            a8 = la_s[i * SLAB:(i + 1) * SLAB, :]
            b8 = lb_s[i * SLAB:(i + 1) * SLAB, :]
            for s in (1, 2, 4):
                m = row8d >= s
                b8 = b8 + a8 * jnp.where(m, pltpu.roll(b8, s, 0), 0.0)
                a8 = a8 * jnp.where(m, pltpu.roll(a8, s, 0), 1.0)
            h8 = a8 * hc + b8
            hc = h8[SLAB - 1:SLAB]
            hs_s[i * SLAB:(i + 1) * SLAB, :] = h8
        h_ref[...] = hc

    def merge_out():
        onorm = onorm_ref[...]
        oa = jnp.concatenate(
            [_head_norm_gate(hg_s[:, h * HEAD_DIM:(h + 1) * HEAD_DIM],
                             og_s[:, h * HEAD_DIM:(h + 1) * HEAD_DIM], onorm)
             for h in range(HEADS)], axis=-1).astype(BF16)
        hb = (ly_s[...] * hs_s[...]).astype(BF16)
        y_ref[0] = _merge_out(x_ref[0], oa, hb, ga_s[...], gb_s[...], g_post_ref[...],
                              w_pa, w_pb, w_o)

    _sample_state_update(step * spb, spb, ss_ref, sft_ref, sq_ref, sv_ref, ss_out_ref, so_ref)
    project_recurrence_inputs()
    rglru_gates()
    project_gates()
    hgrn2_scores()
    hgrn2_recurrence()
    rglru_scan()
    merge_out()

    @pl.when(j == pl.num_programs(1) - 1)
    def _():
        for h in range(HEADS):
            s_out_ref[0, h] = st_ref[h].T
        h_out_ref[0] = h_ref[...]
        c_out_ref[0] = xp_ref[SLAB - (CONV_W - 1):SLAB, :]


def _full(shape):
    return pl.BlockSpec(shape, lambda *_: (0,) * len(shape))


def _prompt_mixer(x, p, tm, s0, ft, q, v):
    nb, t, d = x.shape
    assert t % tm == 0 and tm % CHUNK == 0 and d == HEADS * HEAD_DIM and d % PROJ_COLS == 0
    assert tm // CHUNK <= SLAB and CONV_W - 1 <= SLAB
    nsteps = nb * (t // tm)
    nbs = s0.shape[0]
    spb = nbs // nsteps
    assert spb * nsteps == nbs and nbs == HEAD_DIM
    kern = functools.partial(_prompt_mixer_kernel, tm=tm, spb=spb)
    state_spec = pl.BlockSpec((spb, HEADS, HEAD_DIM, HEAD_DIM),
                              lambda s, j: (s * (t // tm) + j, 0, 0, 0))
    rows_spec = pl.BlockSpec((1, spb, d), lambda s, j: (s * (t // tm) + j, 0, 0))
    q, v = q.reshape(nsteps, spb, d), v.reshape(nsteps, spb, d)
    in_specs = [
        pl.BlockSpec((1, tm, d), lambda s, j: (s, j, 0)),
        _full((1, d)), _full((1, d)), _full(p["lb_logits"].shape),
        _full((1, HEAD_DIM)), _full((CONV_W, d)), _full((1, d)),
        _full(p["wa"].shape), _full((1, d)), _full(p["wx"].shape), _full((1, d)), _full((1, d)),
        state_spec, _full((d, nbs)), rows_spec, rows_spec,
    ]
    weights = []
    for name in ("w_in", "w_pa", "w_pb", "w_o"):
        specs, ops = _col_specs(p[name], PROJ_COLS)
        in_specs += specs
        weights += ops
    out_shape = (
        jax.ShapeDtypeStruct((nb, t, d), F32),
        jax.ShapeDtypeStruct((nb, HEADS, HEAD_DIM, HEAD_DIM), F32),
        jax.ShapeDtypeStruct((nb, 1, d), F32),
        jax.ShapeDtypeStruct((nb, CONV_W - 1, d), F32),
        jax.ShapeDtypeStruct(s0.shape, F32),
        jax.ShapeDtypeStruct((nsteps, spb, d), F32),
    )
    out_specs = (
        pl.BlockSpec((1, tm, d), lambda s, j: (s, j, 0)),
        pl.BlockSpec((1, HEADS, HEAD_DIM, HEAD_DIM), lambda s, j: (s, 0, 0, 0)),
        pl.BlockSpec((1, 1, d), lambda s, j: (s, 0, 0)),
        pl.BlockSpec((1, CONV_W - 1, d), lambda s, j: (s, 0, 0)),
        state_spec, rows_spec,
    )
    tile_f32 = pltpu.VMEM((tm, d), F32)
    tile_bf16 = pltpu.VMEM((tm, d), BF16)
    scratch = [
        pltpu.VMEM((HEADS, HEAD_DIM, HEAD_DIM), F32),
        pltpu.VMEM((1, d), F32),
        pltpu.VMEM((tm + SLAB, d), F32),
        tile_bf16,
        tile_f32, tile_f32, tile_f32,
        tile_bf16,
        tile_f32, tile_f32, tile_f32, tile_f32,
        tile_f32,
        tile_f32,
        tile_bf16,
        tile_bf16,
        pltpu.VMEM((HEADS * (tm // CHUNK), HEAD_DIM, HEAD_DIM), F32),
        pltpu.VMEM((SLAB, d), F32),
        tile_f32, tile_f32,
    ]
    y, s_p, h_p, c_p, s_s, o = pl.pallas_call(
        kern, grid=(nb, t // tm), in_specs=in_specs, out_specs=out_specs, out_shape=out_shape,
        scratch_shapes=scratch, name="prompt_mixer",
        compiler_params=pltpu.CompilerParams(
            dimension_semantics=("arbitrary", "arbitrary"),
            vmem_limit_bytes=V7X_VMEM_LIMIT_BYTES),
    )(x, p["g_mix_pre"], p["g_mix_post"], p["lb_logits"], p["onorm"], p["conv_w"],
      p["conv_b"], p["wa"], p["ba"], p["wx"], p["bx"], p["log_lambda"], s0, ft, q, v, *weights)
    return y, s_p, h_p, c_p, s_s, o.reshape(nbs, d)


def _ffn_kernel(*refs, nf):
    x_ref, g_pre_ref, g_post_ref = refs[:3]
    w_gate, w_up, w_fo = refs[3:3 + nf], refs[3 + nf:3 + 2 * nf], refs[3 + 2 * nf:-1]
    refs[-1][...] = _ffn(x_ref[...], g_pre_ref[...], g_post_ref[...], w_gate, w_up, w_fo)


def _ffn_call(x, p, tm):
    n, d = x.shape
    assert n % tm == 0
    fi_specs, fi_ops = _col_specs(p["w_fi"], FFN_COLS)
    fo_specs, fo_ops = _col_specs(p["w_fo"], PROJ_COLS)
    return pl.pallas_call(
        functools.partial(_ffn_kernel, nf=len(fi_ops) // 2), grid=(n // tm,),
        in_specs=[pl.BlockSpec((tm, d), lambda i: (i, 0)), _full((1, d)), _full((1, d))]
        + fi_specs + fo_specs,
        out_specs=pl.BlockSpec((tm, d), lambda i: (i, 0)),
        out_shape=jax.ShapeDtypeStruct((n, d), F32), name="swiglu",
        compiler_params=pltpu.CompilerParams(
            dimension_semantics=("arbitrary",), vmem_limit_bytes=V7X_VMEM_LIMIT_BYTES),
    )(x, p["g_ffn_pre"], p["g_ffn_post"], *fi_ops, *fo_ops)


def _sample_token_kernel(
        x_ref, h0_ref, cbuf_ref, g_pre_ref, w_in_ref, lbl_ref, cw_ref, cb_ref,
        wa_ref, ba_ref, wx_ref, bx_ref, ll_ref,
        ft_ref, q_ref, v_ref, og_ref, hb_ref, ga_ref, gb_ref, h_out_ref, c_out_ref):
    d = x_ref.shape[-1]
    xn = _rms(x_ref[...], g_pre_ref[...]).astype(BF16)

    def proj(i):
        return _cdot(xn, [w_in_ref.at[:, i * d:(i + 1) * d]])

    lb = _forget_lower_bound(lbl_ref[...])
    qh = _silu(proj(0))
    f = lb + (1.0 - lb) * jax.nn.sigmoid(proj(1))
    for h in range(HEADS):
        sl = slice(h * HEAD_DIM, (h + 1) * HEAD_DIM)
        ft_ref[sl, :] = f[:, sl].T
    q_ref[...] = qh
    v_ref[...] = proj(2)
    og_ref[...] = _silu(proj(3))

    lx = proj(4)
    cw = cw_ref[...]
    taps = [cbuf_ref[k] for k in range(CONV_W - 1)] + [lx]
    xc = cb_ref[...]
    for k in range(CONV_W):
        xc = xc + taps[k] * cw[k:k + 1]
    for k in range(CONV_W - 1):
        c_out_ref[k] = taps[k + 1]
    a, mult, gate_in = _lru_gates(xc, wa_ref, wx_ref, ba_ref[...], bx_ref[...], ll_ref[...])
    hnew = a * h0_ref[...] + mult * gate_in
    h_out_ref[...] = hnew
    hb_ref[...] = (_gelu_tanh(proj(5)) * hnew).astype(BF16)
    ga_ref[...] = jax.nn.sigmoid(proj(6))
    gb_ref[...] = jax.nn.sigmoid(proj(7))


def _sample_tail_kernel(
        x_ref, o_ref, og_ref, hb_ref, ga_ref, gb_ref, onorm_ref, g_post_ref, w_pa_ref, w_pb_ref,
        w_o_ref, g_fpre_ref, g_fpost_ref, w_fi_ref, w_fo_ref, y_ref):
    o = o_ref[...]
    og = og_ref[...]
    oa = jnp.concatenate(
        [_head_norm_gate(o[:, h * HEAD_DIM:(h + 1) * HEAD_DIM],
                         og[:, h * HEAD_DIM:(h + 1) * HEAD_DIM], onorm_ref[...])
         for h in range(HEADS)], axis=-1).astype(BF16)
    x1 = _merge_out(x_ref[...], oa, hb_ref[...], ga_ref[...], gb_ref[...], g_post_ref[...],
                    [w_pa_ref], [w_pb_ref], [w_o_ref])
    d_ff = w_fo_ref.shape[0]
    w_fi = _col_refs(w_fi_ref, d_ff)
    y_ref[...] = _ffn(x1, g_fpre_ref[...], g_fpost_ref[...], w_fi[:1], w_fi[1:], [w_fo_ref])


def _sample_token(x, h0, cbuf, p):
    nb, d = x.shape
    tok = jax.ShapeDtypeStruct((nb, d), F32)
    tok16 = jax.ShapeDtypeStruct((nb, d), BF16)
    col = jax.ShapeDtypeStruct((d, nb), F32)
    return pl.pallas_call(
        _sample_token_kernel,
        out_shape=(col, tok, tok, tok, tok16, tok, tok, tok,
                   jax.ShapeDtypeStruct((CONV_W - 1, nb, d), F32)),
        name="sample_token",
        compiler_params=pltpu.CompilerParams(vmem_limit_bytes=V7X_VMEM_LIMIT_BYTES),
    )(x, h0, cbuf, p["g_mix_pre"], p["w_in"], p["lb_logits"], p["conv_w"], p["conv_b"],
      p["wa"], p["ba"], p["wx"], p["bx"], p["log_lambda"])


def _sample_tail(x, o, og, hb, ga, gb, p):
    return pl.pallas_call(
        _sample_tail_kernel, out_shape=jax.ShapeDtypeStruct(x.shape, F32), name="sample_tail",
        compiler_params=pltpu.CompilerParams(vmem_limit_bytes=V7X_VMEM_LIMIT_BYTES),
    )(x, o, og, hb, ga, gb, p["onorm"], p["g_mix_post"], p["w_pa"], p["w_pb"], p["w_o"],
      p["g_ffn_pre"], p["g_ffn_post"], p["w_fi"], p["w_fo"])


def kernel(x_prompt, x_sample, state_hgrn, state_lru, state_conv, norm_mix_pre, norm_mix_post,
           norm_ffn_pre, norm_ffn_post, w_in, hg_lb_logits, hg_out_norm, lru_conv_w, lru_conv_b,
           lru_w_a, lru_b_a, lru_w_x, lru_b_x, lru_log_lambda, w_branch_a, w_branch_b, w_out,
           w_ffn_in, w_ffn_out):
    depth = w_in.shape[0]
    assert depth == 1 and hg_lb_logits.shape[0] == 2
    nb, t, d = x_prompt.shape
    nbs = x_sample.shape[0]
    assert x_sample.shape[1] == 1

    def row(a):
        return a.reshape(1, -1).astype(F32)

    p = dict(
        g_mix_pre=row(norm_mix_pre[0]), g_mix_post=row(norm_mix_post[0]),
        g_ffn_pre=row(norm_ffn_pre[0]), g_ffn_post=row(norm_ffn_post[0]),
        w_in=w_in[0].astype(BF16), lb_logits=hg_lb_logits.astype(F32),
        onorm=row(hg_out_norm[0]), conv_w=lru_conv_w[0].astype(F32), conv_b=row(lru_conv_b[0]),
        wa=lru_w_a[0].astype(BF16), ba=row(lru_b_a[0]), wx=lru_w_x[0].astype(BF16),
        bx=row(lru_b_x[0]), log_lambda=row(lru_log_lambda[0]),
        w_pa=w_branch_a[0].astype(BF16), w_pb=w_branch_b[0].astype(BF16),
        w_o=w_out[0].astype(BF16), w_fi=w_ffn_in[0].astype(BF16), w_fo=w_ffn_out[0].astype(BF16),
    )

    xs = x_sample.reshape(nbs, d)
    ft, q, v, og, hb, ga, gb, h_s, c_s = _sample_token(
        xs, state_lru[0], jnp.swapaxes(state_conv[0], 0, 1), p)
    x1, s_p, h_p, c_p, s_s, o = _prompt_mixer(x_prompt, p, 256, state_hgrn[0], ft, q, v)
    y_p = _ffn_call(x1.reshape(nb * t, d), p, tm=1024).reshape(nb, t, d)
    y_s = _sample_tail(xs, o, og, hb, ga, gb, p)

    return (y_p, y_s.reshape(nbs, 1, d), s_p[None], h_p.reshape(1, nb, d), c_p[None],
            s_s[None], h_s[None], jnp.swapaxes(c_s, 0, 1)[None])
```

```python
import functools

import jax
import jax.numpy as jnp
from jax import lax
from jax.experimental import pallas as pl
from jax.experimental.pallas import tpu as pltpu

F32 = jnp.float32
BF16 = jnp.bfloat16

HEADS = 8
HEAD_DIM = 128
LRU_BLOCKS = 8
LRU_BLOCK_W = 128
CONV_W = 4
LRU_C = 8.0
EPS = 1e-6
CHUNK = 64
SUB = 16
SLAB = 8
PROJ_COLS = 512
FFN_COLS = 256
V7X_VMEM_LIMIT_BYTES = 56 * 1024 * 1024


def _rms(x, g):
    ms = jnp.mean(x * x, axis=-1, keepdims=True)
    return x * lax.rsqrt(ms + EPS) * g


def _dot_nt(a, b):
    return lax.dot_general(a, b, (((1,), (1,)), ((), ())), preferred_element_type=F32)


def _dot_tn(a, b):
    return lax.dot_general(a, b, (((0,), (0,)), ((), ())), preferred_element_type=F32)


def _silu(x):
    return x * jax.nn.sigmoid(x)


def _gelu_tanh(x):
    c = 0.7978845608028654
    return x * (0.5 * (1.0 + jnp.tanh(c * (x + 0.044715 * (x * x * x)))))


def _softplus(z):
    return jnp.maximum(z, 0.0) + jnp.log1p(jnp.exp(-jnp.abs(z)))


def _forget_lower_bound(logits):
    m = jnp.max(logits, axis=0, keepdims=True)
    e = jnp.exp(logits - m)
    return e[0:1] / (e[0:1] + e[1:2])


def _lru_gates(xc, wa_ref, wx_ref, ba, bx, log_lambda):
    ra, ix = [], []
    for n in range(LRU_BLOCKS):
        sl = slice(n * LRU_BLOCK_W, (n + 1) * LRU_BLOCK_W)
        xb = xc[:, sl].astype(BF16)
        ra.append(jnp.dot(xb, wa_ref[n], preferred_element_type=F32))
        ix.append(jnp.dot(xb, wx_ref[n], preferred_element_type=F32))
    ra = jnp.concatenate(ra, axis=-1) + ba
    ix = jnp.concatenate(ix, axis=-1) + bx
    log_a = (-LRU_C) * jax.nn.sigmoid(ra) * _softplus(-log_lambda)
    a = jnp.exp(log_a)
    mult = jnp.sqrt(-jnp.tanh(log_a) * (a * a + 1.0))
    gate_in = jax.nn.sigmoid(ix) * xc
    return a, mult, gate_in


def _head_norm_gate(oh, og_silu, onorm):
    ms = jnp.mean(oh * oh, axis=-1, keepdims=True)
    return oh * lax.rsqrt(ms + EPS) * onorm * og_silu


def _col_refs(w_ref, cols):
    return [w_ref.at[:, c * cols:(c + 1) * cols] for c in range(w_ref.shape[1] // cols)]


def _col_specs(w, cols):
    n = w.shape[1] // cols
    assert n * cols == w.shape[1]
    specs = [pl.BlockSpec((w.shape[0], cols), functools.partial(lambda c, *_: (0, c), c),
                          pipeline_mode=pl.Buffered(1)) for c in range(n)]
    return specs, [w] * n


def _cdot(a, w_chunks):
    parts = [jnp.dot(a, w[...], preferred_element_type=F32) for w in w_chunks]
    return parts[0] if len(parts) == 1 else jnp.concatenate(parts, axis=-1)


def _merge_out(x, oa, hb, ga_sig, gb_sig, g_post, w_pa, w_pb, w_o):
    merged = ga_sig * _cdot(oa, w_pa) + gb_sig * _cdot(hb, w_pb)
    return x + _rms(_cdot(merged.astype(BF16), w_o), g_post)


def _ffn(x, g_pre, g_post, w_gate, w_up, w_fo):
    hn = _rms(x, g_pre).astype(BF16)
    act = jnp.concatenate(
        [(_silu(_cdot(hn, [wg])) * _cdot(hn, [wu])).astype(BF16)
         for wg, wu in zip(w_gate, w_up, strict=True)], axis=-1)
    return x + _rms(_cdot(act, w_fo), g_post)


def _chunk_prefix_sum(b2_ref, r0, sl, row8):
    slabs, carry = [], None
    for i in range(CHUNK // SLAB):
        vv = b2_ref[r0 + i * SLAB:r0 + (i + 1) * SLAB, sl]
        for s in (1, 2, 4):
            vv = vv + jnp.where(row8 >= s, pltpu.roll(vv, s, 0), 0.0)
        if carry is not None:
            vv = vv + carry
        carry = vv[SLAB - 1:SLAB]
        slabs.append(vv)
    return jnp.concatenate(slabs, axis=0)


def _sample_state_update(first, count, s_ref, ft_ref, q_ref, v_ref, s_out_ref, o_ref):
    lanes = ft_ref.shape[-1]
    shift = lax.rem(lanes - first, lanes)
    ft = pltpu.roll(ft_ref[...], shift, 1)
    for u in range(count):
        for h in range(HEADS):
            sl = slice(h * HEAD_DIM, (h + 1) * HEAD_DIM)
            v_row = v_ref[0, u:u + 1, sl]
            s_new = ft[sl, u:u + 1] * (s_ref[u, h] - v_row) + v_row
            s_out_ref[u, h] = s_new
            o_ref[0, u:u + 1, sl] = jnp.dot(q_ref[0, u:u + 1, sl].astype(BF16),
                                            s_new.astype(BF16), preferred_element_type=F32)


def _prompt_mixer_kernel(*refs, tm, spb):
    (x_ref, g_pre_ref, g_post_ref, lbl_ref, onorm_ref, cw_ref, cb_ref,
     wa_ref, ba_ref, wx_ref, bx_ref, ll_ref, ss_ref, sft_ref, sq_ref, sv_ref) = refs[:16]
    d = x_ref.shape[-1]
    per = d // PROJ_COLS
    weights = refs[16:16 + 11 * per]
    w_in, w_pa, w_pb, w_o = (weights[:8 * per], weights[8 * per:9 * per],
                             weights[9 * per:10 * per], weights[10 * per:])
    (y_ref, s_out_ref, h_out_ref, c_out_ref, ss_out_ref, so_ref,
     st_ref, h_ref, xp_ref, xn_s, qh_s, b2_s, kk_s, v_s, og_s, ly_s, ga_s, gb_s, hg_s, hs_s,
     qin_s, a_s, u_s, dec_s, la_s, lb_s) = refs[16 + 11 * per:]
    j = pl.program_id(1)
    step = pl.program_id(0) * pl.num_programs(1) + j
    nchunk = tm // CHUNK
    nsub = CHUNK // SUB

    @pl.when(j == 0)
    def _():
        st_ref[...] = jnp.zeros_like(st_ref)
        h_ref[...] = jnp.zeros_like(h_ref)
        xp_ref[0:SLAB, :] = jnp.zeros((SLAB, d), F32)

    def proj(g, c0):
        return jnp.dot(xn_s[...], w_in[(g * d + c0) // PROJ_COLS][...],
                       preferred_element_type=F32)

    def project_recurrence_inputs():
        xn_s[...] = _rms(x_ref[0], g_pre_ref[...]).astype(BF16)
        for c0 in range(0, d, PROJ_COLS):
            xp_ref[SLAB:SLAB + tm, c0:c0 + PROJ_COLS] = proj(4, c0)
        lb = _forget_lower_bound(lbl_ref[...])
        for c0 in range(0, d, PROJ_COLS):
            cs = slice(c0, c0 + PROJ_COLS)
            qh_s[:, cs] = _silu(proj(0, c0))
            f = lb[:, cs] + (1.0 - lb[:, cs]) * jax.nn.sigmoid(proj(1, c0))
            b2_s[:, cs] = jnp.log2(f)
            kk_s[:, cs] = 1.0 - f
            v_s[:, cs] = proj(2, c0).astype(BF16)

    def rglru_gates():
        cw = cw_ref[...]
        xc = cb_ref[...]
        for k in range(CONV_W):
            r0 = SLAB - (CONV_W - 1) + k
            xc = xc + xp_ref[r0:r0 + tm, :] * cw[k:k + 1]
        xp_ref[0:SLAB, :] = xp_ref[tm:tm + SLAB, :]
        a, mult, gate_in = _lru_gates(xc, wa_ref, wx_ref, ba_ref[...], bx_ref[...], ll_ref[...])
        row = lax.broadcasted_iota(jnp.int32, (tm, d), 0)
        mult = jnp.where(jnp.logical_and(row == 0, j == 0), 1.0, mult)
        la_s[...] = a
        lb_s[...] = mult * gate_in

    def project_gates():
        for c0 in range(0, d, PROJ_COLS):
            cs = slice(c0, c0 + PROJ_COLS)
            og_s[:, cs] = _silu(proj(3, c0))
            ly_s[:, cs] = _gelu_tanh(proj(5, c0))
            ga_s[:, cs] = jax.nn.sigmoid(proj(6, c0))
            gb_s[:, cs] = jax.nn.sigmoid(proj(7, c0))

    def hgrn2_scores():
        row8 = lax.broadcasted_iota(jnp.int32, (SLAB, HEAD_DIM), 0)
        for h in range(HEADS):
            sl = slice(h * HEAD_DIM, (h + 1) * HEAD_DIM)
            for c in range(nchunk):
                r0 = c * CHUNK
                rs = slice(r0, r0 + CHUNK)
                bc = _chunk_prefix_sum(b2_s, r0, sl, row8)
                qc, kc, vc = qh_s[rs, sl], kk_s[rs, sl], v_s[rs, sl]
                b_last = bc[CHUNK - 1:CHUNK]
                qin_s[rs, sl] = (qc * jnp.exp2(bc)).astype(BF16)
                dec_s[c:c + 1, sl] = jnp.exp2(b_last)
                for ib in range(nsub):
                    t0, t1 = ib * SUB, (ib + 1) * SUB
                    ref_b = bc[t0 + SUB // 2 - 1:t0 + SUB // 2]
                    qs = (qc[t0:t1] * jnp.exp2(bc[t0:t1] - ref_b)).astype(BF16)
                    ks = (kc[:t1] * jnp.exp2(ref_b - bc[:t1])).astype(BF16)
                    a_blk = _dot_nt(qs, ks)
                    ti = lax.broadcasted_iota(jnp.int32, a_blk.shape, 0) + t0
                    si = lax.broadcasted_iota(jnp.int32, a_blk.shape, 1)
                    a_s[r0 + t0:r0 + t1, h * HEAD_DIM:h * HEAD_DIM + t1] = (
                        jnp.where(ti >= si, a_blk, 0.0).astype(BF16))
                kst = (kc * jnp.exp2(b_last - bc)).astype(BF16)
                u_s[h * nchunk + c] = _dot_tn(vc, kst)

    def hgrn2_recurrence():
        for h in range(HEADS):
            sl = slice(h * HEAD_DIM, (h + 1) * HEAD_DIM)
            st = st_ref[h]
            for c in range(nchunk):
                r0 = c * CHUNK
                rs = slice(r0, r0 + CHUNK)
                o_c = _dot_nt(qin_s[rs, sl], st.astype(BF16))
                rows = []
                for ib in range(nsub):
                    t0, t1 = ib * SUB, (ib + 1) * SUB
                    a_blk = a_s[r0 + t0:r0 + t1, h * HEAD_DIM:h * HEAD_DIM + t1]
                    rows.append(o_c[t0:t1] + jnp.dot(a_blk, v_s[r0:r0 + t1, sl],
                                                     preferred_element_type=F32))
                hg_s[rs, sl] = jnp.concatenate(rows, axis=0)
                st = dec_s[c:c + 1, sl] * st + u_s[h * nchunk + c]
            st_ref[h] = st

    def rglru_scan():
        row8d = lax.broadcasted_iota(jnp.int32, (SLAB, d), 0)
        hc = h_ref[...]
        for i in range(tm // SLAB):
            a8 = la_s[i * SLAB:(i + 1) * SLAB, :]
            b8 = lb_s[i * SLAB:(i + 1) * SLAB, :]
            for s in (1, 2, 4):
                m = row8d >= s
                b8 = b8 + a8 * jnp.where(m, pltpu.roll(b8, s, 0), 0.0)
                a8 = a8 * jnp.where(m, pltpu.roll(a8, s, 0), 1.0)
            h8 = a8 * hc + b8
            hc = h8[SLAB - 1:SLAB]
            hs_s[i * SLAB:(i + 1) * SLAB, :] = h8
        h_ref[...] = hc

    def merge_out():
        onorm = onorm_ref[...]
        oa = jnp.concatenate(
            [_head_norm_gate(hg_s[:, h * HEAD_DIM:(h + 1) * HEAD_DIM],
                             og_s[:, h * HEAD_DIM:(h + 1) * HEAD_DIM], onorm)
             for h in range(HEADS)], axis=-1).astype(BF16)
        hb = (ly_s[...] * hs_s[...]).astype(BF16)
        y_ref[0] = _merge_out(x_ref[0], oa, hb, ga_s[...], gb_s[...], g_post_ref[...],
                              w_pa, w_pb, w_o)

    project_recurrence_inputs()
    rglru_gates()
    project_gates()
    hgrn2_scores()
    hgrn2_recurrence()
    rglru_scan()
    merge_out()
    _sample_state_update(step * spb, spb, ss_ref, sft_ref, sq_ref, sv_ref, ss_out_ref, so_ref)

    @pl.when(j == pl.num_programs(1) - 1)
    def _():
        for h in range(HEADS):
            s_out_ref[0, h] = st_ref[h].T
        h_out_ref[0] = h_ref[...]
        c_out_ref[0] = xp_ref[SLAB - (CONV_W - 1):SLAB, :]


def _full(shape):
    return pl.BlockSpec(shape, lambda *_: (0,) * len(shape))


def _prompt_mixer(x, p, tm, s0, ft, q, v):
    nb, t, d = x.shape
    assert t % tm == 0 and tm % CHUNK == 0 and d == HEADS * HEAD_DIM and d % PROJ_COLS == 0
    assert tm // CHUNK <= SLAB and CONV_W - 1 <= SLAB
    nsteps = nb * (t // tm)
    nbs = s0.shape[0]
    spb = nbs // nsteps
    assert spb * nsteps == nbs and nbs == HEAD_DIM
    kern = functools.partial(_prompt_mixer_kernel, tm=tm, spb=spb)
    state_spec = pl.BlockSpec((spb, HEADS, HEAD_DIM, HEAD_DIM),
                              lambda s, j: (s * (t // tm) + j, 0, 0, 0))
    rows_spec = pl.BlockSpec((1, spb, d), lambda s, j: (s * (t // tm) + j, 0, 0))
    q, v = q.reshape(nsteps, spb, d), v.reshape(nsteps, spb, d)
    in_specs = [
        pl.BlockSpec((1, tm, d), lambda s, j: (s, j, 0)),
        _full((1, d)), _full((1, d)), _full(p["lb_logits"].shape),
        _full((1, HEAD_DIM)), _full((CONV_W, d)), _full((1, d)),
        _full(p["wa"].shape), _full((1, d)), _full(p["wx"].shape), _full((1, d)), _full((1, d)),
        state_spec, _full((d, nbs)), rows_spec, rows_spec,
    ]
    weights = []
    for name in ("w_in", "w_pa", "w_pb", "w_o"):
        specs, ops = _col_specs(p[name], PROJ_COLS)
        in_specs += specs
        weights += ops
    out_shape = (
        jax.ShapeDtypeStruct((nb, t, d), F32),
        jax.ShapeDtypeStruct((nb, HEADS, HEAD_DIM, HEAD_DIM), F32),
        jax.ShapeDtypeStruct((nb, 1, d), F32),
        jax.ShapeDtypeStruct((nb, CONV_W - 1, d), F32),
        jax.ShapeDtypeStruct(s0.shape, F32),
        jax.ShapeDtypeStruct((nsteps, spb, d), F32),
    )
    out_specs = (
        pl.BlockSpec((1, tm, d), lambda s, j: (s, j, 0)),
        pl.BlockSpec((1, HEADS, HEAD_DIM, HEAD_DIM), lambda s, j: (s, 0, 0, 0)),
        pl.BlockSpec((1, 1, d), lambda s, j: (s, 0, 0)),
        pl.BlockSpec((1, CONV_W - 1, d), lambda s, j: (s, 0, 0)),
        state_spec, rows_spec,
    )
    tile_f32 = pltpu.VMEM((tm, d), F32)
    tile_bf16 = pltpu.VMEM((tm, d), BF16)
    scratch = [
        pltpu.VMEM((HEADS, HEAD_DIM, HEAD_DIM), F32),
        pltpu.VMEM((1, d), F32),
        pltpu.VMEM((tm + SLAB, d), F32),
        tile_bf16,
        tile_f32, tile_f32, tile_f32,
        tile_bf16,
        tile_f32, tile_f32, tile_f32, tile_f32,
        tile_f32,
        tile_f32,
        tile_bf16,
        tile_bf16,
        pltpu.VMEM((HEADS * (tm // CHUNK), HEAD_DIM, HEAD_DIM), F32),
        pltpu.VMEM((SLAB, d), F32),
        tile_f32, tile_f32,
    ]
    y, s_p, h_p, c_p, s_s, o = pl.pallas_call(
        kern, grid=(nb, t // tm), in_specs=in_specs, out_specs=out_specs, out_shape=out_shape,
        scratch_shapes=scratch, name="prompt_mixer",
        compiler_params=pltpu.CompilerParams(
            dimension_semantics=("arbitrary", "arbitrary"),
            vmem_limit_bytes=V7X_VMEM_LIMIT_BYTES),
    )(x, p["g_mix_pre"], p["g_mix_post"], p["lb_logits"], p["onorm"], p["conv_w"],
      p["conv_b"], p["wa"], p["ba"], p["wx"], p["bx"], p["log_lambda"], s0, ft, q, v, *weights)
    return y, s_p, h_p, c_p, s_s, o.reshape(nbs, d)


def _ffn_kernel(*refs, nf):
    x_ref, g_pre_ref, g_post_ref = refs[:3]
    w_gate, w_up, w_fo = refs[3:3 + nf], refs[3 + nf:3 + 2 * nf], refs[3 + 2 * nf:-1]
    refs[-1][...] = _ffn(x_ref[...], g_pre_ref[...], g_post_ref[...], w_gate, w_up, w_fo)


def _ffn_call(x, p, tm):
    n, d = x.shape
    assert n % tm == 0
    fi_specs, fi_ops = _col_specs(p["w_fi"], FFN_COLS)
    fo_specs, fo_ops = _col_specs(p["w_fo"], PROJ_COLS)
    return pl.pallas_call(
        functools.partial(_ffn_kernel, nf=len(fi_ops) // 2), grid=(n // tm,),
        in_specs=[pl.BlockSpec((tm, d), lambda i: (i, 0)), _full((1, d)), _full((1, d))]
        + fi_specs + fo_specs,
        out_specs=pl.BlockSpec((tm, d), lambda i: (i, 0)),
        out_shape=jax.ShapeDtypeStruct((n, d), F32), name="swiglu",
        compiler_params=pltpu.CompilerParams(
            dimension_semantics=("arbitrary",), vmem_limit_bytes=V7X_VMEM_LIMIT_BYTES),
    )(x, p["g_ffn_pre"], p["g_ffn_post"], *fi_ops, *fo_ops)


def _sample_token_kernel(
        x_ref, h0_ref, cbuf_ref, g_pre_ref, w_in_ref, lbl_ref, cw_ref, cb_ref,
        wa_ref, ba_ref, wx_ref, bx_ref, ll_ref,
        ft_ref, q_ref, v_ref, og_ref, hb_ref, ga_ref, gb_ref, h_out_ref, c_out_ref):
    d = x_ref.shape[-1]
    xn = _rms(x_ref[...], g_pre_ref[...]).astype(BF16)

    def proj(i):
        return _cdot(xn, [w_in_ref.at[:, i * d:(i + 1) * d]])

    lb = _forget_lower_bound(lbl_ref[...])
    qh = _silu(proj(0))
    f = lb + (1.0 - lb) * jax.nn.sigmoid(proj(1))
    for h in range(HEADS):
        sl = slice(h * HEAD_DIM, (h + 1) * HEAD_DIM)
        ft_ref[sl, :] = f[:, sl].T
    q_ref[...] = qh
    v_ref[...] = proj(2)
    og_ref[...] = _silu(proj(3))

    lx = proj(4)
    cw = cw_ref[...]
    taps = [cbuf_ref[k] for k in range(CONV_W - 1)] + [lx]
    xc = cb_ref[...]
    for k in range(CONV_W):
        xc = xc + taps[k] * cw[k:k + 1]
    for k in range(CONV_W - 1):
        c_out_ref[k] = taps[k + 1]
    a, mult, gate_in = _lru_gates(xc, wa_ref, wx_ref, ba_ref[...], bx_ref[...], ll_ref[...])
    hnew = a * h0_ref[...] + mult * gate_in
    h_out_ref[...] = hnew
    hb_ref[...] = (_gelu_tanh(proj(5)) * hnew).astype(BF16)
    ga_ref[...] = jax.nn.sigmoid(proj(6))
    gb_ref[...] = jax.nn.sigmoid(proj(7))


def _sample_tail_kernel(
        x_ref, o_ref, og_ref, hb_ref, ga_ref, gb_ref, onorm_ref, g_post_ref, w_pa_ref, w_pb_ref,
        w_o_ref, g_fpre_ref, g_fpost_ref, w_fi_ref, w_fo_ref, y_ref):
    o = o_ref[...]
    og = og_ref[...]
    oa = jnp.concatenate(
        [_head_norm_gate(o[:, h * HEAD_DIM:(h + 1) * HEAD_DIM],
                         og[:, h * HEAD_DIM:(h + 1) * HEAD_DIM], onorm_ref[...])
         for h in range(HEADS)], axis=-1).astype(BF16)
    x1 = _merge_out(x_ref[...], oa, hb_ref[...], ga_ref[...], gb_ref[...], g_post_ref[...],
                    [w_pa_ref], [w_pb_ref], [w_o_ref])
    d_ff = w_fo_ref.shape[0]
    w_fi = _col_refs(w_fi_ref, d_ff)
    y_ref[...] = _ffn(x1, g_fpre_ref[...], g_fpost_ref[...], w_fi[:1], w_fi[1:], [w_fo_ref])


def _sample_token(x, h0, cbuf, p):
    nb, d = x.shape
    tok = jax.ShapeDtypeStruct((nb, d), F32)
    tok16 = jax.ShapeDtypeStruct((nb, d), BF16)
    col = jax.ShapeDtypeStruct((d, nb), F32)
    return pl.pallas_call(
        _sample_token_kernel,
        out_shape=(col, tok, tok, tok, tok16, tok, tok, tok,
                   jax.ShapeDtypeStruct((CONV_W - 1, nb, d), F32)),
        name="sample_token",
        compiler_params=pltpu.CompilerParams(vmem_limit_bytes=V7X_VMEM_LIMIT_BYTES),
    )(x, h0, cbuf, p["g_mix_pre"], p["w_in"], p["lb_logits"], p["conv_w"], p["conv_b"],
      p["wa"], p["ba"], p["wx"], p["bx"], p["log_lambda"])


def _sample_tail(x, o, og, hb, ga, gb, p):
    return pl.pallas_call(
        _sample_tail_kernel, out_shape=jax.ShapeDtypeStruct(x.shape, F32), name="sample_tail",
        compiler_params=pltpu.CompilerParams(vmem_limit_bytes=V7X_VMEM_LIMIT_BYTES),
    )(x, o, og, hb, ga, gb, p["onorm"], p["g_mix_post"], p["w_pa"], p["w_pb"], p["w_o"],
      p["g_ffn_pre"], p["g_ffn_post"], p["w_fi"], p["w_fo"])


def kernel(x_prompt, x_sample, state_hgrn, state_lru, state_conv, norm_mix_pre, norm_mix_post,
           norm_ffn_pre, norm_ffn_post, w_in, hg_lb_logits, hg_out_norm, lru_conv_w, lru_conv_b,
           lru_w_a, lru_b_a, lru_w_x, lru_b_x, lru_log_lambda, w_branch_a, w_branch_b, w_out,
           w_ffn_in, w_ffn_out):
    depth = w_in.shape[0]
    assert depth == 1 and hg_lb_logits.shape[0] == 2
    nb, t, d = x_prompt.shape
    nbs = x_sample.shape[0]
    assert x_sample.shape[1] == 1

    def row(a):
        return a.reshape(1, -1).astype(F32)

    p = dict(
        g_mix_pre=row(norm_mix_pre[0]), g_mix_post=row(norm_mix_post[0]),
        g_ffn_pre=row(norm_ffn_pre[0]), g_ffn_post=row(norm_ffn_post[0]),
        w_in=w_in[0].astype(BF16), lb_logits=hg_lb_logits.astype(F32),
        onorm=row(hg_out_norm[0]), conv_w=lru_conv_w[0].astype(F32), conv_b=row(lru_conv_b[0]),
        wa=lru_w_a[0].astype(BF16), ba=row(lru_b_a[0]), wx=lru_w_x[0].astype(BF16),
        bx=row(lru_b_x[0]), log_lambda=row(lru_log_lambda[0]),
        w_pa=w_branch_a[0].astype(BF16), w_pb=w_branch_b[0].astype(BF16),
        w_o=w_out[0].astype(BF16), w_fi=w_ffn_in[0].astype(BF16), w_fo=w_ffn_out[0].astype(BF16),
    )

    xs = x_sample.reshape(nbs, d)
    ft, q, v, og, hb, ga, gb, h_s, c_s = _sample_token(
        xs, state_lru[0], jnp.swapaxes(state_conv[0], 0, 1), p)
    x1, s_p, h_p, c_p, s_s, o = _prompt_mixer(x_prompt, p, 256, state_hgrn[0], ft, q, v)
    y_p = _ffn_call(x1.reshape(nb * t, d), p, tm=1024).reshape(nb, t, d)
    y_s = _sample_tail(xs, o, og, hb, ga, gb, p)

    return (y_p, y_s.reshape(nbs, 1, d), s_p[None], h_p.reshape(1, nb, d), c_p[None],
            s_s[None], h_s[None], jnp.swapaxes(c_s, 0, 1)[None])
```

```python
import functools

import jax
import jax.numpy as jnp
from jax import lax
from jax.experimental import pallas as pl
from jax.experimental.pallas import tpu as pltpu

F32 = jnp.float32
BF16 = jnp.bfloat16

HEADS = 8
HEAD_DIM = 128
LRU_BLOCKS = 8
LRU_BLOCK_W = 128
CONV_W = 4
LRU_C = 8.0
EPS = 1e-6
CHUNK = 64
SUB = 16
SLAB = 8
PROJ_COLS = 512
FFN_COLS = 256
V7X_VMEM_LIMIT_BYTES = 56 * 1024 * 1024


def _rms(x, g):
    ms = jnp.mean(x * x, axis=-1, keepdims=True)
    return x * lax.rsqrt(ms + EPS) * g


def _dot_nt(a, b):
    return lax.dot_general(a, b, (((1,), (1,)), ((), ())), preferred_element_type=F32)


def _dot_tn(a, b):
    return lax.dot_general(a, b, (((0,), (0,)), ((), ())), preferred_element_type=F32)


def _silu(x):
    return x * jax.nn.sigmoid(x)


def _gelu_tanh(x):
    c = 0.7978845608028654
    return x * (0.5 * (1.0 + jnp.tanh(c * (x + 0.044715 * (x * x * x)))))


def _softplus(z):
    return jnp.maximum(z, 0.0) + jnp.log1p(jnp.exp(-jnp.abs(z)))


def _forget_lower_bound(logits):
    m = jnp.max(logits, axis=0, keepdims=True)
    e = jnp.exp(logits - m)
    return e[0:1] / (e[0:1] + e[1:2])


def _lru_gates(xc, wa_ref, wx_ref, ba, bx, log_lambda):
    ra, ix = [], []
    for n in range(LRU_BLOCKS):
        sl = slice(n * LRU_BLOCK_W, (n + 1) * LRU_BLOCK_W)
        xb = xc[:, sl].astype(BF16)
        ra.append(jnp.dot(xb, wa_ref[n], preferred_element_type=F32))
        ix.append(jnp.dot(xb, wx_ref[n], preferred_element_type=F32))
    ra = jnp.concatenate(ra, axis=-1) + ba
    ix = jnp.concatenate(ix, axis=-1) + bx
    log_a = (-LRU_C) * jax.nn.sigmoid(ra) * _softplus(-log_lambda)
    a = jnp.exp(log_a)
    mult = jnp.sqrt(-jnp.tanh(log_a) * (a * a + 1.0))
    gate_in = jax.nn.sigmoid(ix) * xc
    return a, mult, gate_in


def _head_norm_gate(oh, og_silu, onorm):
    ms = jnp.mean(oh * oh, axis=-1, keepdims=True)
    return oh * lax.rsqrt(ms + EPS) * onorm * og_silu


def _col_refs(w_ref, cols):
    return [w_ref.at[:, c * cols:(c + 1) * cols] for c in range(w_ref.shape[1] // cols)]


def _col_specs(w, cols):
    n = w.shape[1] // cols
    assert n * cols == w.shape[1]
    specs = [pl.BlockSpec((w.shape[0], cols), functools.partial(lambda c, *_: (0, c), c),
                          pipeline_mode=pl.Buffered(1)) for c in range(n)]
    return specs, [w] * n


def _cdot(a, w_chunks):
    parts = [jnp.dot(a, w[...], preferred_element_type=F32) for w in w_chunks]
    return parts[0] if len(parts) == 1 else jnp.concatenate(parts, axis=-1)


def _merge_out(x, oa, hb, ga_sig, gb_sig, g_post, w_pa, w_pb, w_o):
    merged = ga_sig * _cdot(oa, w_pa) + gb_sig * _cdot(hb, w_pb)
    return x + _rms(_cdot(merged.astype(BF16), w_o), g_post)


def _ffn(x, g_pre, g_post, w_gate, w_up, w_fo):
    hn = _rms(x, g_pre).astype(BF16)
    act = jnp.concatenate(
        [(_silu(_cdot(hn, [wg])) * _cdot(hn, [wu])).astype(BF16)
         for wg, wu in zip(w_gate, w_up, strict=True)], axis=-1)
    return x + _rms(_cdot(act, w_fo), g_post)


def _chunk_prefix_sum(b2_ref, r0, sl, row8):
    slabs, carry = [], None
    for i in range(CHUNK // SLAB):
        vv = b2_ref[r0 + i * SLAB:r0 + (i + 1) * SLAB, sl]
        for s in (1, 2, 4):
            vv = vv + jnp.where(row8 >= s, pltpu.roll(vv, s, 0), 0.0)
        if carry is not None:
            vv = vv + carry
        carry = vv[SLAB - 1:SLAB]
        slabs.append(vv)
    return jnp.concatenate(slabs, axis=0)


def _sample_state_update(first, count, s_ref, ft_ref, q_ref, v_ref, s_out_ref, o_ref):
    lanes = ft_ref.shape[-1]
    shift = lax.rem(lanes - first, lanes)
    ft = pltpu.roll(ft_ref[...], shift, 1)
    for u in range(count):
        for h in range(HEADS):
            sl = slice(h * HEAD_DIM, (h + 1) * HEAD_DIM)
            v_row = v_ref[0, u:u + 1, sl]
            s_new = ft[sl, u:u + 1] * (s_ref[u, h] - v_row) + v_row
            s_out_ref[u, h] = s_new
            o_ref[0, u:u + 1, sl] = jnp.dot(q_ref[0, u:u + 1, sl].astype(BF16),
                                            s_new.astype(BF16), preferred_element_type=F32)


def _prompt_mixer_kernel(*refs, tm, spb):
    (x_ref, g_pre_ref, g_post_ref, lbl_ref, onorm_ref, cw_ref, cb_ref,
     wa_ref, ba_ref, wx_ref, bx_ref, ll_ref, ss_ref, sft_ref, sq_ref, sv_ref,
     wfi_f32_ref, wfo_f32_ref) = refs[:18]
    d = x_ref.shape[-1]
    per = d // PROJ_COLS
    weights = refs[18:18 + 11 * per]
    w_in, w_pa, w_pb, w_o = (weights[:8 * per], weights[8 * per:9 * per],
                             weights[9 * per:10 * per], weights[10 * per:])
    (y_ref, s_out_ref, h_out_ref, c_out_ref, ss_out_ref, so_ref, wfi_out_ref, wfo_out_ref,
     st_ref, h_ref, xp_ref, xn_s, qh_s, b2_s, kk_s, v_s, og_s, ly_s, ga_s, gb_s, hg_s, hs_s,
     qin_s, a_s, u_s, dec_s, la_s, lb_s) = refs[18 + 11 * per:]
    j = pl.program_id(1)
    step = pl.program_id(0) * pl.num_programs(1) + j
    nchunk = tm // CHUNK
    nsub = CHUNK // SUB

    @pl.when(j == 0)
    def _():
        st_ref[...] = jnp.zeros_like(st_ref)
        h_ref[...] = jnp.zeros_like(h_ref)
        xp_ref[0:SLAB, :] = jnp.zeros((SLAB, d), F32)

    def proj(g, c0):
        return jnp.dot(xn_s[...], w_in[(g * d + c0) // PROJ_COLS][...],
                       preferred_element_type=F32)

    def project_recurrence_inputs():
        xn_s[...] = _rms(x_ref[0], g_pre_ref[...]).astype(BF16)
        for c0 in range(0, d, PROJ_COLS):
            xp_ref[SLAB:SLAB + tm, c0:c0 + PROJ_COLS] = proj(4, c0)
        lb = _forget_lower_bound(lbl_ref[...])
        for c0 in range(0, d, PROJ_COLS):
            cs = slice(c0, c0 + PROJ_COLS)
            qh_s[:, cs] = _silu(proj(0, c0))
            f = lb[:, cs] + (1.0 - lb[:, cs]) * jax.nn.sigmoid(proj(1, c0))
            b2_s[:, cs] = jnp.log2(f)
            kk_s[:, cs] = 1.0 - f
            v_s[:, cs] = proj(2, c0).astype(BF16)

    def rglru_gates():
        cw = cw_ref[...]
        xc = cb_ref[...]
        for k in range(CONV_W):
            r0 = SLAB - (CONV_W - 1) + k
            xc = xc + xp_ref[r0:r0 + tm, :] * cw[k:k + 1]
        xp_ref[0:SLAB, :] = xp_ref[tm:tm + SLAB, :]
        a, mult, gate_in = _lru_gates(xc, wa_ref, wx_ref, ba_ref[...], bx_ref[...], ll_ref[...])
        row = lax.broadcasted_iota(jnp.int32, (tm, d), 0)
        mult = jnp.where(jnp.logical_and(row == 0, j == 0), 1.0, mult)
        la_s[...] = a
        lb_s[...] = mult * gate_in

    def project_gates():
        for c0 in range(0, d, PROJ_COLS):
            cs = slice(c0, c0 + PROJ_COLS)
            og_s[:, cs] = _silu(proj(3, c0))
            ly_s[:, cs] = _gelu_tanh(proj(5, c0))
            ga_s[:, cs] = jax.nn.sigmoid(proj(6, c0))
            gb_s[:, cs] = jax.nn.sigmoid(proj(7, c0))

    def hgrn2_scores():
        row8 = lax.broadcasted_iota(jnp.int32, (SLAB, HEAD_DIM), 0)
        for h in range(HEADS):
            sl = slice(h * HEAD_DIM, (h + 1) * HEAD_DIM)
            for c in range(nchunk):
                r0 = c * CHUNK
                rs = slice(r0, r0 + CHUNK)
                bc = _chunk_prefix_sum(b2_s, r0, sl, row8)
                qc, kc, vc = qh_s[rs, sl], kk_s[rs, sl], v_s[rs, sl]
                b_last = bc[CHUNK - 1:CHUNK]
                qin_s[rs, sl] = (qc * jnp.exp2(bc)).astype(BF16)
                dec_s[c:c + 1, sl] = jnp.exp2(b_last)
                for ib in range(nsub):
                    t0, t1 = ib * SUB, (ib + 1) * SUB
                    ref_b = bc[t0 + SUB // 2 - 1:t0 + SUB // 2]
                    qs = (qc[t0:t1] * jnp.exp2(bc[t0:t1] - ref_b)).astype(BF16)
                    ks = (kc[:t1] * jnp.exp2(ref_b - bc[:t1])).astype(BF16)
                    a_blk = _dot_nt(qs, ks)
                    ti = lax.broadcasted_iota(jnp.int32, a_blk.shape, 0) + t0
                    si = lax.broadcasted_iota(jnp.int32, a_blk.shape, 1)
                    a_s[r0 + t0:r0 + t1, h * HEAD_DIM:h * HEAD_DIM + t1] = (
                        jnp.where(ti >= si, a_blk, 0.0).astype(BF16))
                kst = (kc * jnp.exp2(b_last - bc)).astype(BF16)
                u_s[h * nchunk + c] = _dot_tn(vc, kst)

    def hgrn2_recurrence():
        for h in range(HEADS):
            sl = slice(h * HEAD_DIM, (h + 1) * HEAD_DIM)
            st = st_ref[h]
            for c in range(nchunk):
                r0 = c * CHUNK
                rs = slice(r0, r0 + CHUNK)
                o_c = _dot_nt(qin_s[rs, sl], st.astype(BF16))
                rows = []
                for ib in range(nsub):
                    t0, t1 = ib * SUB, (ib + 1) * SUB
                    a_blk = a_s[r0 + t0:r0 + t1, h * HEAD_DIM:h * HEAD_DIM + t1]
                    rows.append(o_c[t0:t1] + jnp.dot(a_blk, v_s[r0:r0 + t1, sl],
                                                     preferred_element_type=F32))
                hg_s[rs, sl] = jnp.concatenate(rows, axis=0)
                st = dec_s[c:c + 1, sl] * st + u_s[h * nchunk + c]
            st_ref[h] = st

    def rglru_scan():
        row8d = lax.broadcasted_iota(jnp.int32, (SLAB, d), 0)
        hc = h_ref[...]
        for i in range(tm // SLAB):
            a8 = la_s[i * SLAB:(i + 1) * SLAB, :]
            b8 = lb_s[i * SLAB:(i + 1) * SLAB, :]
            for s in (1, 2, 4):
                m = row8d >= s
                b8 = b8 + a8 * jnp.where(m, pltpu.roll(b8, s, 0), 0.0)
                a8 = a8 * jnp.where(m, pltpu.roll(a8, s, 0), 1.0)
            h8 = a8 * hc + b8
            hc = h8[SLAB - 1:SLAB]
            hs_s[i * SLAB:(i + 1) * SLAB, :] = h8
        h_ref[...] = hc

    def merge_out():
        onorm = onorm_ref[...]
        oa = jnp.concatenate(
            [_head_norm_gate(hg_s[:, h * HEAD_DIM:(h + 1) * HEAD_DIM],
                             og_s[:, h * HEAD_DIM:(h + 1) * HEAD_DIM], onorm)
             for h in range(HEADS)], axis=-1).astype(BF16)
        hb = (ly_s[...] * hs_s[...]).astype(BF16)
        y_ref[0] = _merge_out(x_ref[0], oa, hb, ga_s[...], gb_s[...], g_post_ref[...],
                              w_pa, w_pb, w_o)

    project_recurrence_inputs()
    rglru_gates()
    project_gates()
    hgrn2_scores()
    hgrn2_recurrence()
    rglru_scan()
    merge_out()
    _sample_state_update(step * spb, spb, ss_ref, sft_ref, sq_ref, sv_ref, ss_out_ref, so_ref)
    wfi_out_ref[...] = wfi_f32_ref[...].astype(BF16)
    wfo_out_ref[...] = wfo_f32_ref[...].astype(BF16)

    @pl.when(j == pl.num_programs(1) - 1)
    def _():
        for h in range(HEADS):
            s_out_ref[0, h] = st_ref[h].T
        h_out_ref[0] = h_ref[...]
        c_out_ref[0] = xp_ref[SLAB - (CONV_W - 1):SLAB, :]


def _full(shape):
    return pl.BlockSpec(shape, lambda *_: (0,) * len(shape))


def _cast_spec(w, outer, inner, steps_per_block):
    nblk = outer * inner // steps_per_block
    rows = w.shape[0] // nblk
    assert nblk * steps_per_block == outer * inner and rows * nblk == w.shape[0]
    assert rows % 16 == 0
    return pl.BlockSpec((rows, w.shape[1]),
                        lambda s, j: ((s * inner + j) // steps_per_block, 0))


def _prompt_mixer(x, p, tm, s0, ft, q, v, w_fi_f32, w_fo_f32):
    nb, t, d = x.shape
    assert t % tm == 0 and tm % CHUNK == 0 and d == HEADS * HEAD_DIM and d % PROJ_COLS == 0
    assert tm // CHUNK <= SLAB and CONV_W - 1 <= SLAB
    nsteps = nb * (t // tm)
    nbs = s0.shape[0]
    spb = nbs // nsteps
    assert spb * nsteps == nbs and nbs == HEAD_DIM
    kern = functools.partial(_prompt_mixer_kernel, tm=tm, spb=spb)
    state_spec = pl.BlockSpec((spb, HEADS, HEAD_DIM, HEAD_DIM),
                              lambda s, j: (s * (t // tm) + j, 0, 0, 0))
    rows_spec = pl.BlockSpec((1, spb, d), lambda s, j: (s * (t // tm) + j, 0, 0))
    q, v = q.reshape(nsteps, spb, d), v.reshape(nsteps, spb, d)
    in_specs = [
        pl.BlockSpec((1, tm, d), lambda s, j: (s, j, 0)),
        _full((1, d)), _full((1, d)), _full(p["lb_logits"].shape),
        _full((1, HEAD_DIM)), _full((CONV_W, d)), _full((1, d)),
        _full(p["wa"].shape), _full((1, d)), _full(p["wx"].shape), _full((1, d)), _full((1, d)),
        state_spec, _full((d, nbs)), rows_spec, rows_spec,
    ]

    def steps_per_block(rows):
        return next(g for g in range(1, nsteps + 1)
                    if nsteps % g == 0 and rows % (nsteps // g) == 0
                    and (rows // (nsteps // g)) % 16 == 0)

    cast_specs = [_cast_spec(w, nb, t // tm, steps_per_block(w.shape[0]))
                  for w in (w_fi_f32, w_fo_f32)]
    in_specs += cast_specs
    weights = []
    for name in ("w_in", "w_pa", "w_pb", "w_o"):
        specs, ops = _col_specs(p[name], PROJ_COLS)
        in_specs += specs
        weights += ops
    out_shape = (
        jax.ShapeDtypeStruct((nb, t, d), F32),
        jax.ShapeDtypeStruct((nb, HEADS, HEAD_DIM, HEAD_DIM), F32),
        jax.ShapeDtypeStruct((nb, 1, d), F32),
        jax.ShapeDtypeStruct((nb, CONV_W - 1, d), F32),
        jax.ShapeDtypeStruct(s0.shape, F32),
        jax.ShapeDtypeStruct((nsteps, spb, d), F32),
        jax.ShapeDtypeStruct(w_fi_f32.shape, BF16),
        jax.ShapeDtypeStruct(w_fo_f32.shape, BF16),
    )
    out_specs = (
        pl.BlockSpec((1, tm, d), lambda s, j: (s, j, 0)),
        pl.BlockSpec((1, HEADS, HEAD_DIM, HEAD_DIM), lambda s, j: (s, 0, 0, 0)),
        pl.BlockSpec((1, 1, d), lambda s, j: (s, 0, 0)),
        pl.BlockSpec((1, CONV_W - 1, d), lambda s, j: (s, 0, 0)),
        state_spec, rows_spec, *cast_specs,
    )
    tile_f32 = pltpu.VMEM((tm, d), F32)
    tile_bf16 = pltpu.VMEM((tm, d), BF16)
    scratch = [
        pltpu.VMEM((HEADS, HEAD_DIM, HEAD_DIM), F32),
        pltpu.VMEM((1, d), F32),
        pltpu.VMEM((tm + SLAB, d), F32),
        tile_bf16,
        tile_f32, tile_f32, tile_f32,
        tile_bf16,
        tile_f32, tile_f32, tile_f32, tile_f32,
        tile_f32,
        tile_f32,
        tile_bf16,
        tile_bf16,
        pltpu.VMEM((HEADS * (tm // CHUNK), HEAD_DIM, HEAD_DIM), F32),
        pltpu.VMEM((SLAB, d), F32),
        tile_f32, tile_f32,
    ]
    y, s_p, h_p, c_p, s_s, o, w_fi, w_fo = pl.pallas_call(
        kern, grid=(nb, t // tm), in_specs=in_specs, out_specs=out_specs, out_shape=out_shape,
        scratch_shapes=scratch, name="prompt_mixer",
        compiler_params=pltpu.CompilerParams(
            dimension_semantics=("arbitrary", "arbitrary"),
            vmem_limit_bytes=V7X_VMEM_LIMIT_BYTES),
    )(x, p["g_mix_pre"], p["g_mix_post"], p["lb_logits"], p["onorm"], p["conv_w"],
      p["conv_b"], p["wa"], p["ba"], p["wx"], p["bx"], p["log_lambda"], s0, ft, q, v,
      w_fi_f32, w_fo_f32, *weights)
    return y, s_p, h_p, c_p, s_s, o.reshape(nbs, d), w_fi, w_fo


def _ffn_kernel(*refs, nf):
    x_ref, g_pre_ref, g_post_ref = refs[:3]
    w_gate, w_up, w_fo = refs[3:3 + nf], refs[3 + nf:3 + 2 * nf], refs[3 + 2 * nf:-1]
    refs[-1][...] = _ffn(x_ref[...], g_pre_ref[...], g_post_ref[...], w_gate, w_up, w_fo)


def _ffn_call(x, p, tm):
    n, d = x.shape
    assert n % tm == 0
    fi_specs, fi_ops = _col_specs(p["w_fi"], FFN_COLS)
    fo_specs, fo_ops = _col_specs(p["w_fo"], PROJ_COLS)
    return pl.pallas_call(
        functools.partial(_ffn_kernel, nf=len(fi_ops) // 2), grid=(n // tm,),
        in_specs=[pl.BlockSpec((tm, d), lambda i: (i, 0)), _full((1, d)), _full((1, d))]
        + fi_specs + fo_specs,
        out_specs=pl.BlockSpec((tm, d), lambda i: (i, 0)),
        out_shape=jax.ShapeDtypeStruct((n, d), F32), name="swiglu",
        compiler_params=pltpu.CompilerParams(
            dimension_semantics=("arbitrary",), vmem_limit_bytes=V7X_VMEM_LIMIT_BYTES),
    )(x, p["g_ffn_pre"], p["g_ffn_post"], *fi_ops, *fo_ops)


def _sample_token_kernel(
        x_ref, h0_ref, cbuf_ref, g_pre_ref, w_in_ref, lbl_ref, cw_ref, cb_ref,
        wa_ref, ba_ref, wx_ref, bx_ref, ll_ref,
        ft_ref, q_ref, v_ref, og_ref, hb_ref, ga_ref, gb_ref, h_out_ref, c_out_ref):
    d = x_ref.shape[-1]
    xn = _rms(x_ref[...], g_pre_ref[...]).astype(BF16)

    def proj(i):
        return _cdot(xn, [w_in_ref.at[:, i * d:(i + 1) * d]])

    lb = _forget_lower_bound(lbl_ref[...])
    qh = _silu(proj(0))
    f = lb + (1.0 - lb) * jax.nn.sigmoid(proj(1))
    for h in range(HEADS):
        sl = slice(h * HEAD_DIM, (h + 1) * HEAD_DIM)
        ft_ref[sl, :] = f[:, sl].T
    q_ref[...] = qh
    v_ref[...] = proj(2)
    og_ref[...] = _silu(proj(3))

    lx = proj(4)
    cw = cw_ref[...]
    taps = [cbuf_ref[k] for k in range(CONV_W - 1)] + [lx]
    xc = cb_ref[...]
    for k in range(CONV_W):
        xc = xc + taps[k] * cw[k:k + 1]
    for k in range(CONV_W - 1):
        c_out_ref[k] = taps[k + 1]
    a, mult, gate_in = _lru_gates(xc, wa_ref, wx_ref, ba_ref[...], bx_ref[...], ll_ref[...])
    hnew = a * h0_ref[...] + mult * gate_in
    h_out_ref[...] = hnew
    hb_ref[...] = (_gelu_tanh(proj(5)) * hnew).astype(BF16)
    ga_ref[...] = jax.nn.sigmoid(proj(6))
    gb_ref[...] = jax.nn.sigmoid(proj(7))


def _sample_tail_kernel(
        x_ref, o_ref, og_ref, hb_ref, ga_ref, gb_ref, onorm_ref, g_post_ref, w_pa_ref, w_pb_ref,
        w_o_ref, g_fpre_ref, g_fpost_ref, w_fi_ref, w_fo_ref, y_ref):
    o = o_ref[...]
    og = og_ref[...]
    oa = jnp.concatenate(
        [_head_norm_gate(o[:, h * HEAD_DIM:(h + 1) * HEAD_DIM],
                         og[:, h * HEAD_DIM:(h + 1) * HEAD_DIM], onorm_ref[...])
         for h in range(HEADS)], axis=-1).astype(BF16)
    x1 = _merge_out(x_ref[...], oa, hb_ref[...], ga_ref[...], gb_ref[...], g_post_ref[...],
                    [w_pa_ref], [w_pb_ref], [w_o_ref])
    d_ff = w_fo_ref.shape[0]
    w_fi = _col_refs(w_fi_ref, d_ff)
    y_ref[...] = _ffn(x1, g_fpre_ref[...], g_fpost_ref[...], w_fi[:1], w_fi[1:], [w_fo_ref])


def _sample_token(x, h0, cbuf, p):
    nb, d = x.shape
    tok = jax.ShapeDtypeStruct((nb, d), F32)
    tok16 = jax.ShapeDtypeStruct((nb, d), BF16)
    col = jax.ShapeDtypeStruct((d, nb), F32)
    return pl.pallas_call(
        _sample_token_kernel,
        out_shape=(col, tok, tok, tok, tok16, tok, tok, tok,
                   jax.ShapeDtypeStruct((CONV_W - 1, nb, d), F32)),
        name="sample_token",
        compiler_params=pltpu.CompilerParams(vmem_limit_bytes=V7X_VMEM_LIMIT_BYTES),
    )(x, h0, cbuf, p["g_mix_pre"], p["w_in"], p["lb_logits"], p["conv_w"], p["conv_b"],
      p["wa"], p["ba"], p["wx"], p["bx"], p["log_lambda"])


def _sample_tail(x, o, og, hb, ga, gb, p):
    return pl.pallas_call(
        _sample_tail_kernel, out_shape=jax.ShapeDtypeStruct(x.shape, F32), name="sample_tail",
        compiler_params=pltpu.CompilerParams(vmem_limit_bytes=V7X_VMEM_LIMIT_BYTES),
    )(x, o, og, hb, ga, gb, p["onorm"], p["g_mix_post"], p["w_pa"], p["w_pb"], p["w_o"],
      p["g_ffn_pre"], p["g_ffn_post"], p["w_fi"], p["w_fo"])


def kernel(x_prompt, x_sample, state_hgrn, state_lru, state_conv, norm_mix_pre, norm_mix_post,
           norm_ffn_pre, norm_ffn_post, w_in, hg_lb_logits, hg_out_norm, lru_conv_w, lru_conv_b,
           lru_w_a, lru_b_a, lru_w_x, lru_b_x, lru_log_lambda, w_branch_a, w_branch_b, w_out,
           w_ffn_in, w_ffn_out):
    depth = w_in.shape[0]
    assert depth == 1 and hg_lb_logits.shape[0] == 2
    nb, t, d = x_prompt.shape
    nbs = x_sample.shape[0]
    assert x_sample.shape[1] == 1

    def row(a):
        return a.reshape(1, -1).astype(F32)

    p = dict(
        g_mix_pre=row(norm_mix_pre[0]), g_mix_post=row(norm_mix_post[0]),
        g_ffn_pre=row(norm_ffn_pre[0]), g_ffn_post=row(norm_ffn_post[0]),
        w_in=w_in[0].astype(BF16), lb_logits=hg_lb_logits.astype(F32),
        onorm=row(hg_out_norm[0]), conv_w=lru_conv_w[0].astype(F32), conv_b=row(lru_conv_b[0]),
        wa=lru_w_a[0].astype(BF16), ba=row(lru_b_a[0]), wx=lru_w_x[0].astype(BF16),
        bx=row(lru_b_x[0]), log_lambda=row(lru_log_lambda[0]),
        w_pa=w_branch_a[0].astype(BF16), w_pb=w_branch_b[0].astype(BF16),
        w_o=w_out[0].astype(BF16),
    )

    xs = x_sample.reshape(nbs, d)
    ft, q, v, og, hb, ga, gb, h_s, c_s = _sample_token(
        xs, state_lru[0], jnp.swapaxes(state_conv[0], 0, 1), p)
    x1, s_p, h_p, c_p, s_s, o, p["w_fi"], p["w_fo"] = _prompt_mixer(
        x_prompt, p, 256, state_hgrn[0], ft, q, v,
        w_ffn_in[0].astype(F32), w_ffn_out[0].astype(F32))
    y_p = _ffn_call(x1.reshape(nb * t, d), p, tm=1024).reshape(nb, t, d)
    y_s = _sample_tail(xs, o, og, hb, ga, gb, p)

    return (y_p, y_s.reshape(nbs, 1, d), s_p[None], h_p.reshape(1, nb, d), c_p[None],
            s_s[None], h_s[None], jnp.swapaxes(c_s, 0, 1)[None])
```
